```python
import jax, jax.numpy as jnp
from jax import lax
import numpy as np

D_MODEL = 1024
BATCH = 16
SEQ = 2048
DEPTH = 1

CTX_LEN = 256
GRID_W = 64
D_MIX = 1024
HEAD_DIM = 64
N_Q_HEADS = 8
N_KV_HEADS = 2
Q_PER_KV = N_Q_HEADS // N_KV_HEADS
D_ATTN = N_Q_HEADS * HEAD_DIM
D_KV = N_KV_HEADS * HEAD_DIM
N_GM_GROUPS = 8
GM_GROUP_DIM = 64
D_GM = N_GM_GROUPS * GM_GROUP_DIM
CHUNK = 128
Q_BLOCK = 128
D_IN = D_ATTN + 2 * D_KV + 2 * D_GM
ROPE_THETA = 10000.0
N_EXPERTS = 16
CAPACITY_FACTOR = 2
EXPERT_FF = 2048
EPS = 1e-6

kernel_name = "hybrid_gqa_gmlp_ecmoe_dit_layer"


def rmsnorm(x, g):
    xf = x.astype(jnp.float32)
    y = xf * lax.rsqrt(jnp.mean(xf * xf, axis=-1, keepdims=True) + EPS)
    return (y * g.astype(jnp.float32)).astype(x.dtype)


def adaln(cvec, w_mod, b_mod):
    m = jax.nn.silu(cvec) @ w_mod + b_mod
    return jnp.split(m, 6, axis=-1)


def axial_rope(row, col):
    n_freq = HEAD_DIM // 4
    inv = ROPE_THETA ** (-jnp.arange(n_freq, dtype=jnp.float32) / n_freq)
    ang = jnp.concatenate([row[:, None].astype(jnp.float32) * inv,
                           col[:, None].astype(jnp.float32) * inv], axis=-1)
    return jnp.cos(ang), jnp.sin(ang)


def apply_rope(x, cos, sin):
    shape = (cos.shape[0],) + (1,) * (x.ndim - 3) + (cos.shape[-1],)
    cos = cos.reshape(shape).astype(x.dtype)
    sin = sin.reshape(shape).astype(x.dtype)
    x1, x2 = jnp.split(x, 2, axis=-1)
    return jnp.concatenate([x1 * cos - x2 * sin, x1 * sin + x2 * cos], axis=-1)


def modulate(x, g, shift, scale):
    return rmsnorm(x, g) * (1 + scale) + shift


def split_in_proj(w_in):
    return jnp.split(w_in, [D_ATTN, D_ATTN + 2 * D_KV], axis=1)


def queries(h, w_q, q_gain):
    b, n = h.shape[:2]
    return rmsnorm((h @ w_q).reshape(b, n, N_KV_HEADS, Q_PER_KV, HEAD_DIM), q_gain)


def keys_values(h, w_kv, k_gain):
    b, n = h.shape[:2]
    k, v = jnp.split(h @ w_kv, 2, axis=-1)
    k = rmsnorm(k.reshape(b, n, N_KV_HEADS, HEAD_DIM), k_gain)
    return k, v.reshape(b, n, N_KV_HEADS, HEAD_DIM)


def gqa_attend(q, k, v):
    b, nq = q.shape[:2]
    scale = HEAD_DIM ** -0.5

    def block(qb):
        s = jnp.einsum('bqkgd,bskd->bkgqs', qb, k).astype(jnp.float32) * scale
        p = jax.nn.softmax(s, axis=-1).astype(v.dtype)
        return jnp.einsum('bkgqs,bskd->bqkgd', p, v)

    nb = nq // Q_BLOCK
    qb = jnp.moveaxis(q.reshape(b, nb, Q_BLOCK, N_KV_HEADS, Q_PER_KV, HEAD_DIM), 1, 0)
    o = lax.map(block, qb)
    return jnp.moveaxis(o, 0, 1).reshape(b, nq, D_ATTN)


def spatial_gating(h, w_gm, v_gain, w_s, b_s):
    b, n = h.shape[:2]
    u, vv = jnp.split(jax.nn.gelu(h @ w_gm, approximate=False), 2, axis=-1)
    vv = rmsnorm(vv, v_gain).reshape(b, n // CHUNK, CHUNK, N_GM_GROUPS, GM_GROUP_DIM)
    mixed = jnp.einsum('gij,bcjgd->bcigd', w_s, vv) + b_s.T[:, :, None]
    return u * mixed.reshape(b, n, D_GM)


def expert_choice_ffn(h, w_router, w1, w3, w2):
    n, d = h.shape[1], h.shape[2]
    cap = CAPACITY_FACTOR * n // N_EXPERTS
    aff = jax.nn.softmax((h @ w_router).astype(jnp.float32), axis=-1)
    gate, idx = lax.top_k(jnp.swapaxes(aff, 1, 2), cap)
    xe = jax.vmap(lambda hb, ib: hb[ib])(h, idx)
    hid = jax.nn.silu(jnp.einsum('becd,edf->becf', xe, w1)) * jnp.einsum('becd,edf->becf', xe, w3)
    ye = jnp.einsum('becf,efd->becd', hid, w2) * gate[..., None].astype(h.dtype)

    def scatter(yb, ib):
        return jnp.zeros((n, d), h.dtype).at[ib.reshape(-1)].add(yb.reshape(-1, d))

    return jax.vmap(scatter)(ye, idx)


def hybrid_layer(xl, xc, c, c_ctx, cos, sin, w_mod, b_mod, g_mix, g_ffn, w_in, q_gain, k_gain,
                 v_gain, w_s, b_s, w_out, w_router, w1, w3, w2, update_ctx):
    sh1, sc1, gt1, sh2, sc2, gt2 = [m[:, None, :] for m in adaln(c, w_mod, b_mod)]
    csh1, csc1, cgt1, csh2, csc2, cgt2 = adaln(c_ctx, w_mod, b_mod)
    w_q, w_kv, w_gm = split_in_proj(w_in)

    hl = modulate(xl, g_mix, sh1, sc1)
    hc = modulate(xc, g_mix, csh1, csc1)
    kc, vc = keys_values(hc, w_kv, k_gain)
    ql = apply_rope(queries(hl, w_q, q_gain), cos, sin)
    kl, vl = keys_values(hl, w_kv, k_gain)
    kl = apply_rope(kl, cos, sin)
    k_all = jnp.concatenate([kl, kc], axis=1)
    v_all = jnp.concatenate([vl, vc], axis=1)
    mix_l = jnp.concatenate([gqa_attend(ql, k_all, v_all),
                             spatial_gating(hl, w_gm, v_gain, w_s, b_s)], axis=-1)
    xl_new = xl + gt1 * (mix_l @ w_out)

    xl_new = xl_new + gt2 * expert_choice_ffn(modulate(xl_new, g_ffn, sh2, sc2), w_router, w1, w3, w2)

    if update_ctx:
        qc = queries(hc, w_q, q_gain)
        mix_c = jnp.concatenate([gqa_attend(qc, kc, vc),
                                 spatial_gating(hc, w_gm, v_gain, w_s, b_s)], axis=-1)
        xc = xc + cgt1 * (mix_c @ w_out)
        xc = xc + cgt2 * expert_choice_ffn(modulate(xc, g_ffn, csh2, csc2), w_router, w1, w3, w2)
    return xl_new, xc


def setup_inputs(seed: int = 0) -> dict:
    key = jax.random.key(seed)
    ks = jax.random.split(key, 21)
    f32 = jnp.float32
    L = DEPTH

    def nrm(k, shape, s):
        return jax.random.normal(k, shape, f32) * s

    return {
        "x": nrm(ks[0], (BATCH, SEQ, D_MODEL), 1.0),
        "c": nrm(ks[1], (BATCH, D_MODEL), 1.0),
        "ctx": nrm(ks[2], (BATCH, CTX_LEN, D_MODEL), 1.0),
        "c_ctx": nrm(ks[3], (D_MODEL,), 1.0),
        "w_mod": nrm(ks[4], (L, D_MODEL, 6 * D_MODEL), D_MODEL ** -0.5),
        "b_mod": nrm(ks[5], (L, 6 * D_MODEL), 0.02),
        "g_mix": 1.0 + nrm(ks[6], (L, D_MODEL), 0.02),
        "g_ffn": 1.0 + nrm(ks[7], (L, D_MODEL), 0.02),
        "w_in": nrm(ks[8], (L, D_MODEL, D_IN), D_MODEL ** -0.5),
        "q_gain": 1.0 + nrm(ks[9], (L, HEAD_DIM), 0.02),
        "k_gain": 1.0 + nrm(ks[10], (L, HEAD_DIM), 0.02),
        "v_gain": 1.0 + nrm(ks[11], (L, D_GM), 0.02),
        "w_s": nrm(ks[12], (L, N_GM_GROUPS, CHUNK, CHUNK), CHUNK ** -0.5),
        "b_s": 1.0 + nrm(ks[13], (L, N_GM_GROUPS, CHUNK), 0.02),
        "w_out": nrm(ks[14], (L, D_MIX, D_MODEL), D_MIX ** -0.5),
        "w_router": nrm(ks[15], (L, D_MODEL, N_EXPERTS), D_MODEL ** -0.5),
        "w1": nrm(ks[16], (L, N_EXPERTS, D_MODEL, EXPERT_FF), D_MODEL ** -0.5),
        "w3": nrm(ks[17], (L, N_EXPERTS, D_MODEL, EXPERT_FF), D_MODEL ** -0.5),
        "w2": nrm(ks[18], (L, N_EXPERTS, EXPERT_FF, D_MODEL), EXPERT_FF ** -0.5),
        "g_final": 1.0 + nrm(ks[19], (D_MODEL,), 0.02),
    }


def reference(x, c, ctx, c_ctx, w_mod, b_mod, g_mix, g_ffn, w_in, q_gain, k_gain, v_gain,
              w_s, b_s, w_out, w_router, w1, w3, w2, g_final):
    n_lat = x.shape[1]
    rows = n_lat // GRID_W
    row = jnp.repeat(jnp.arange(rows), GRID_W)
    col = jnp.tile(jnp.arange(GRID_W), rows)
    cos, sin = axial_rope(row, col)
    xl, xc = x, ctx
    for layer in range(DEPTH):
        xl, xc = hybrid_layer(xl, xc, c, c_ctx, cos, sin, w_mod[layer], b_mod[layer], g_mix[layer],
                              g_ffn[layer], w_in[layer], q_gain[layer], k_gain[layer], v_gain[layer],
                              w_s[layer], b_s[layer], w_out[layer], w_router[layer], w1[layer],
                              w3[layer], w2[layer], update_ctx=layer + 1 < DEPTH)
    return rmsnorm(xl, g_final)
```

```python
import functools
import math

import jax
import jax.numpy as jnp
from jax import lax
from jax.experimental import pallas as pl
from jax.experimental.pallas import tpu as pltpu

D_MODEL = 1024
CTX_LEN = 256
GRID_W = 64
HEAD_DIM = 64
N_Q_HEADS = 8
N_KV_HEADS = 2
Q_PER_KV = N_Q_HEADS // N_KV_HEADS
D_ATTN = N_Q_HEADS * HEAD_DIM
D_KV = N_KV_HEADS * HEAD_DIM
N_GM_GROUPS = 8
GM_GROUP_DIM = 64
D_GM = N_GM_GROUPS * GM_GROUP_DIM
CHUNK = 128
D_IN = D_ATTN + 2 * D_KV + 2 * D_GM
ROPE_THETA = 10000.0
N_EXPERTS = 16
CAPACITY_FACTOR = 2
EXPERT_FF = 2048
EPS = 1e-6

F32 = jnp.float32
BF16 = jnp.bfloat16

VMEM_LIMIT_BYTES = 52 * 1024 * 1024

K_OFF = D_ATTN
V_OFF = D_ATTN + D_KV
GM_OFF = D_ATTN + 2 * D_KV

Q_SCALE = math.log2(math.e) * HEAD_DIM ** -0.5


def _params(*sem):
    return pltpu.CompilerParams(dimension_semantics=sem, vmem_limit_bytes=VMEM_LIMIT_BYTES)


def _dot(a, b):
    return jnp.dot(a, b, preferred_element_type=F32)


def _dot_nt(a, b):
    return lax.dot_general(a, b, (((1,), (1,)), ((), ())), preferred_element_type=F32)


def _rms(v):
    return lax.rsqrt(jnp.sum(v * v, axis=-1, keepdims=True) * (1.0 / v.shape[-1]) + EPS)


def _modulate(x, g, shift, scale):
    return ((x * _rms(x)) * g) * (1.0 + scale) + shift


def _seg_sum(v, bd):
    w = bd.shape[0]
    outs = []
    for j in range(v.shape[1] // w):
        c = v[:, j * w:(j + 1) * w]
        hi = c.astype(BF16)
        lo = (c - hi.astype(F32)).astype(BF16)
        outs.append(_dot(hi, bd) + _dot(lo, bd))
    return outs[0] if len(outs) == 1 else jnp.concatenate(outs, axis=-1)


def _head_norm(z, bd, gain):
    ss = _seg_sum(z * z, bd)
    return (z * lax.rsqrt(ss * (1.0 / HEAD_DIM) + EPS)) * gain


def _rope(v, cos, sin_signed):
    w = v.shape[-1]
    lane = lax.broadcasted_iota(jnp.int32, v.shape, 1)
    partner = jnp.where((lane & (HEAD_DIM // 2)) == 0,
                        pltpu.roll(v, w - HEAD_DIM // 2, axis=1),
                        pltpu.roll(v, HEAD_DIM // 2, axis=1))
    return v * cos + partner * sin_signed


def _gelu(z):
    return 0.5 * z * (1.0 + lax.erf(z * math.sqrt(0.5)))


def _adaln_kernel(c_ref, w_ref, b_ref, o_ref):
    c = c_ref[...]
    s = c / (1.0 + jnp.exp(-c))
    o_ref[...] = _dot(s.astype(BF16), w_ref[...].astype(BF16)) + b_ref[...]


def _adaln(cvecs, w_mod, b_mod):
    rows = cvecs.shape[0]
    n_out = w_mod.shape[1]
    tn = 768
    return pl.pallas_call(
        _adaln_kernel,
        grid=(n_out // tn,),
        in_specs=[pl.BlockSpec((rows, D_MODEL), lambda j: (0, 0)),
                  pl.BlockSpec((D_MODEL, tn), lambda j: (0, j)),
                  pl.BlockSpec((1, tn), lambda j: (0, j))],
        out_specs=pl.BlockSpec((rows, tn), lambda j: (0, j)),
        out_shape=jax.ShapeDtypeStruct((rows, n_out), F32),
        compiler_params=_params("arbitrary"),
        name="adaln",
    )(cvecs, w_mod, b_mod.reshape(1, n_out))


def _inproj_kernel(x_ref, sh_ref, sc_ref, g_ref, w_ref, bd_ref, qg_ref, kg_ref, vg_ref,
                   cos_ref, sin_ref, q_ref, k_ref, v_ref, u_ref, vv_ref):
    h = _modulate(x_ref[0], g_ref[...], sh_ref[0], sc_ref[0])
    z = _dot(h.astype(BF16), w_ref[...])
    cos = cos_ref[...]
    sin = sin_ref[...]
    bd = bd_ref[...]

    q = _rope(_head_norm(z[:, :D_ATTN], bd, qg_ref[...]), cos, sin)
    q_ref[0] = (q * Q_SCALE).astype(BF16)

    k = _head_norm(z[:, K_OFF:K_OFF + D_KV], bd[:D_KV, :D_KV], kg_ref[...])
    k_ref[0] = _rope(k, cos[:, :D_KV], sin[:, :D_KV]).astype(BF16)
    v_ref[0] = z[:, V_OFF:V_OFF + D_KV].astype(BF16)

    gz = _gelu(z[:, GM_OFF:])
    u_ref[0] = gz[:, :D_GM]
    vv = gz[:, D_GM:]
    vv_ref[0] = ((vv * _rms(vv)) * vg_ref[...]).astype(BF16)


def _inproj(x, sh1, sc1, g_mix, w_in, bd, qg, kg, vg, cos_t, sin_t, tile):
    b, n, _ = x.shape
    row = lambda i, j: (j, i, 0)
    per_b = lambda i, j: (j, 0, 0)
    const = lambda i, j: (0, 0)
    tab = lambda i, j: (i, 0)
    return pl.pallas_call(
        _inproj_kernel,
        grid=(n // tile, b),
        in_specs=[pl.BlockSpec((1, tile, D_MODEL), row),
                  pl.BlockSpec((1, 1, D_MODEL), per_b),
                  pl.BlockSpec((1, 1, D_MODEL), per_b),
                  pl.BlockSpec((1, D_MODEL), const),
                  pl.BlockSpec((D_MODEL, D_IN), const),
                  pl.BlockSpec(bd.shape, const),
                  pl.BlockSpec((1, D_ATTN), const),
                  pl.BlockSpec((1, D_KV), const),
                  pl.BlockSpec((1, D_GM), const),
                  pl.BlockSpec((tile, D_ATTN), tab),
                  pl.BlockSpec((tile, D_ATTN), tab)],
        out_specs=[pl.BlockSpec((1, tile, D_ATTN), row),
                   pl.BlockSpec((1, tile, D_KV), row),
                   pl.BlockSpec((1, tile, D_KV), row),
                   pl.BlockSpec((1, tile, D_GM), row),
                   pl.BlockSpec((1, tile, D_GM), row)],
        out_shape=[jax.ShapeDtypeStruct((b, n, D_ATTN), BF16),
                   jax.ShapeDtypeStruct((b, n, D_KV), BF16),
                   jax.ShapeDtypeStruct((b, n, D_KV), BF16),
                   jax.ShapeDtypeStruct((b, n, D_GM), F32),
                   jax.ShapeDtypeStruct((b, n, D_GM), BF16)],
        compiler_params=_params("arbitrary", "arbitrary"),
        name="inproj",
    )(x, sh1, sc1, g_mix, w_in, bd, qg, kg, vg, cos_t, sin_t)


def _ctxkv_kernel(x_ref, sh_ref, sc_ref, g_ref, w_ref, bd_ref, kg_ref, k_ref, v_ref):
    h = _modulate(x_ref[0], g_ref[...], sh_ref[...], sc_ref[...])
    z = _dot(h.astype(BF16), w_ref[:, K_OFF:K_OFF + 2 * D_KV])
    k_ref[0] = _head_norm(z[:, :D_KV], bd_ref[:D_KV, :D_KV], kg_ref[...]).astype(BF16)
    v_ref[0] = z[:, D_KV:].astype(BF16)


def _ctxkv(ctx, csh1, csc1, g_mix, w_in, bd, kg):
    b, n, _ = ctx.shape
    const = lambda i: (0, 0)
    row = lambda i: (i, 0, 0)
    return pl.pallas_call(
        _ctxkv_kernel,
        grid=(b,),
        in_specs=[pl.BlockSpec((1, n, D_MODEL), row),
                  pl.BlockSpec((1, D_MODEL), const),
                  pl.BlockSpec((1, D_MODEL), const),
                  pl.BlockSpec((1, D_MODEL), const),
                  pl.BlockSpec((D_MODEL, D_IN), const),
                  pl.BlockSpec(bd.shape, const),
                  pl.BlockSpec((1, D_KV), const)],
        out_specs=[pl.BlockSpec((1, n, D_KV), row),
                   pl.BlockSpec((1, n, D_KV), row)],
        out_shape=[jax.ShapeDtypeStruct((b, n, D_KV), BF16),
                   jax.ShapeDtypeStruct((b, n, D_KV), BF16)],
        compiler_params=_params("arbitrary"),
        name="ctxkv",
    )(ctx, csh1, csc1, g_mix, w_in, bd, kg)


ATTN_LOOKAHEAD = 4


def _attn_kernel(q_ref, k_ref, vt_ref, o_ref, st_ref):
    q = q_ref[0]
    k = k_ref[0]
    vt = vt_ref[0]
    def scores(h):
        kv = h // Q_PER_KV
        qh = q[:, h * HEAD_DIM:(h + 1) * HEAD_DIM]
        kh = k[:, kv * HEAD_DIM:(kv + 1) * HEAD_DIM]
        st_ref[h] = _dot_nt(kh, qh)

    for h in range(ATTN_LOOKAHEAD):
        scores(h)
    outs = []
    for h in range(N_Q_HEADS):
        if h + ATTN_LOOKAHEAD < N_Q_HEADS:
            scores(h + ATTN_LOOKAHEAD)
        kv = h // Q_PER_KV
        st = st_ref[h]
        pt = jnp.exp2(st - jnp.max(st, axis=0, keepdims=True))
        l = jnp.sum(pt, axis=0, keepdims=True)
        outs.append(_dot(vt[kv * HEAD_DIM:(kv + 1) * HEAD_DIM, :], pt.astype(BF16)) / l)
    o_ref[0] = jnp.concatenate(outs, axis=0).T.astype(BF16)


def _attn(q, k_all, vt_all, tile):
    b, n, _ = q.shape
    nk = k_all.shape[1]
    return pl.pallas_call(
        _attn_kernel,
        grid=(b, n // tile),
        in_specs=[pl.BlockSpec((1, tile, D_ATTN), lambda i, j: (i, j, 0)),
                  pl.BlockSpec((1, nk, D_KV), lambda i, j: (i, 0, 0)),
                  pl.BlockSpec((1, D_KV, nk), lambda i, j: (i, 0, 0))],
        out_specs=pl.BlockSpec((1, tile, D_ATTN), lambda i, j: (i, j, 0)),
        out_shape=jax.ShapeDtypeStruct((b, n, D_ATTN), BF16),
        scratch_shapes=[pltpu.VMEM((N_Q_HEADS, nk, tile), F32)],
        compiler_params=_params("arbitrary", "arbitrary"),
        name="attn",
    )(q, k_all, vt_all)


def _mixout_kernel(o_ref, u_ref, vv_ref, x_ref, ws_ref, bs_ref, wout_ref, gt_ref, g_ref,
                   sh_ref, sc_ref, wr_ref, xn_ref, h_ref, aff_ref):
    tile = o_ref.shape[1]
    seg = lax.broadcasted_iota(jnp.int32, (CHUNK, D_GM), 1) // GM_GROUP_DIM
    bs = bs_ref[...]
    gms = []
    for c in range(tile // CHUNK):
        vc = vv_ref[0, c * CHUNK:(c + 1) * CHUNK, :]
        mixed = jnp.zeros((CHUNK, D_GM), F32)
        for g in range(N_GM_GROUPS):
            mixed = jnp.where(seg == g, _dot(ws_ref[g], vc), mixed)
        gms.append(u_ref[0, c * CHUNK:(c + 1) * CHUNK, :] * (mixed + bs))
    gm = jnp.concatenate(gms, axis=0).astype(BF16)

    proj = _dot(o_ref[0], wout_ref[:D_ATTN, :]) + _dot(gm, wout_ref[D_ATTN:, :])
    xn = x_ref[0] + gt_ref[0] * proj
    xn_ref[0] = xn

    h = _modulate(xn, g_ref[...], sh_ref[0], sc_ref[0]).astype(BF16)
    h_ref[0] = h
    logits = _dot_nt(wr_ref[...], h)
    e = jnp.exp(logits - jnp.max(logits, axis=0, keepdims=True))
    aff_ref[0] = e / jnp.sum(e, axis=0, keepdims=True)


def _mixout(o, u, vvn, x, w_s, bs, w_out, gt1, g_ffn, sh2, sc2, w_rt, tile):
    b, n, _ = x.shape
    row = lambda i, j: (i, j, 0)
    per_b = lambda i, j: (i, 0, 0)
    const2 = lambda i, j: (0, 0)
    return pl.pallas_call(
        _mixout_kernel,
        grid=(b, n // tile),
        in_specs=[pl.BlockSpec((1, tile, D_ATTN), row),
                  pl.BlockSpec((1, tile, D_GM), row),
                  pl.BlockSpec((1, tile, D_GM), row),
                  pl.BlockSpec((1, tile, D_MODEL), row),
                  pl.BlockSpec((N_GM_GROUPS, CHUNK, CHUNK), lambda i, j: (0, 0, 0)),
                  pl.BlockSpec((CHUNK, D_GM), const2),
                  pl.BlockSpec((D_ATTN + D_GM, D_MODEL), const2),
                  pl.BlockSpec((1, 1, D_MODEL), per_b),
                  pl.BlockSpec((1, D_MODEL), const2),
                  pl.BlockSpec((1, 1, D_MODEL), per_b),
                  pl.BlockSpec((1, 1, D_MODEL), per_b),
                  pl.BlockSpec((N_EXPERTS, D_MODEL), const2)],
        out_specs=[pl.BlockSpec((1, tile, D_MODEL), row),
                   pl.BlockSpec((1, tile, D_MODEL), row),
                   pl.BlockSpec((1, N_EXPERTS, tile), lambda i, j: (i, 0, j))],
        out_shape=[jax.ShapeDtypeStruct((b, n, D_MODEL), F32),
                   jax.ShapeDtypeStruct((b, n, D_MODEL), BF16),
                   jax.ShapeDtypeStruct((b, N_EXPERTS, n), F32)],
        compiler_params=_params("arbitrary", "arbitrary"),
        name="mixout",
    )(o, u, vvn, x, w_s, bs, w_out, gt1, g_ffn, sh2, sc2, w_rt)


def _cumsum_lanes(m, tri):
    off = jnp.zeros((m.shape[0], 1), F32)
    outs = []
    for blk in range(m.shape[1] // 128):
        mb = m[:, blk * 128:(blk + 1) * 128]
        outs.append(_dot(mb.astype(BF16), tri) + off)
        off = off + jnp.sum(mb, axis=-1, keepdims=True)
    return jnp.concatenate(outs, axis=-1)


REFINE_STEPS = 32


def _route_kernel(cap, aff_ref, tri_ref, pos_ref):
    a = aff_ref[...]
    capf = float(cap)

    def count_ge(t):
        return jnp.sum(jnp.where(a >= t, 1.0, 0.0), axis=-1, keepdims=True)

    def bit_step(i, prefix):
        cand = prefix | jnp.left_shift(jnp.int32(1), 30 - i)
        return jnp.where(count_ge(pltpu.bitcast(cand, F32)) >= capf, cand, prefix)

    prefix = lax.fori_loop(0, 31, bit_step, jnp.zeros((a.shape[0], 1), jnp.int32))
    lo = pltpu.bitcast(prefix, F32)
    hi = pltpu.bitcast(prefix + 1, F32)

    def refine(_, carry):
        lo, hi = carry
        mid = 0.5 * lo + 0.5 * hi
        ok = count_ge(mid) >= capf
        return jnp.where(ok, mid, lo), jnp.where(ok, hi, mid)

    lo, hi = lax.fori_loop(0, REFINE_STEPS, refine, (lo, hi))
    thr = jnp.min(jnp.where(a >= lo, a, jnp.inf), axis=-1, keepdims=True)

    gt = jnp.where(a > thr, 1.0, 0.0)
    eq = jnp.where(a == thr, 1.0, 0.0)
    need = capf - jnp.sum(gt, axis=-1, keepdims=True)
    tri = tri_ref[...]
    sel = gt + eq * jnp.where(_cumsum_lanes(eq, tri) <= need, 1.0, 0.0)
    slot = _cumsum_lanes(sel, tri) - 1.0
    pos_ref[...] = jnp.where(sel > 0.0, slot, -1.0).astype(jnp.int32)


def _route(aff_t, tri, cap, rows):
    b, e, n = aff_t.shape
    pos = pl.pallas_call(
        functools.partial(_route_kernel, cap),
        grid=(b * e // rows,),
        in_specs=[pl.BlockSpec((rows, n), lambda i: (i, 0)),
                  pl.BlockSpec((128, 128), lambda i: (0, 0))],
        out_specs=pl.BlockSpec((rows, n), lambda i: (i, 0)),
        out_shape=jax.ShapeDtypeStruct((b * e, n), jnp.int32),
        compiler_params=_params("arbitrary"),
        name="route",
    )(aff_t.reshape(b * e, n), tri)
    return pos.reshape(b, e, n)


def _gather_kernel(cap, pos_ref, h_ref, xe_ref):
    pos = pos_ref[0, 0]
    slot = lax.broadcasted_iota(jnp.int32, (cap, pos.shape[-1]), 0)
    onehot = jnp.where(pos == slot, 1.0, 0.0).astype(BF16)
    xe_ref[0] = _dot(onehot, h_ref[0]).astype(BF16)


def _gather(pos_t, h2, cap):
    b, e, n = pos_t.shape
    return pl.pallas_call(
        functools.partial(_gather_kernel, cap),
        grid=(b, e),
        in_specs=[pl.BlockSpec((1, 1, 1, n), lambda i, j: (i, j, 0, 0)),
                  pl.BlockSpec((1, n, D_MODEL), lambda i, j: (i, 0, 0))],
        out_specs=pl.BlockSpec((1, cap, D_MODEL), lambda i, j: (j, i, 0)),
        out_shape=jax.ShapeDtypeStruct((e, b * cap, D_MODEL), BF16),
        compiler_params=_params("arbitrary", "arbitrary"),
        name="gather",
    )(pos_t.reshape(b, e, 1, n), h2)


FF_CHUNK = 512


def _ffn_kernel(n_experts, xe_ref, w1_ref, w3_ref, w2_ref, y_ref, w1b, w3b, w2b):
    i = pl.program_id(0)
    j = pl.program_id(1)
    rows13 = w1_ref.shape[1]
    rows2 = w2_ref.shape[1]

    @pl.when(i < n_experts)
    def _():
        slot = i % 2
        w1b[slot, pl.ds(j * rows13, rows13), :] = w1_ref[0].astype(BF16)
        w3b[slot, pl.ds(j * rows13, rows13), :] = w3_ref[0].astype(BF16)
        w2b[slot, pl.ds(j * rows2, rows2), :] = w2_ref[0].astype(BF16)

    @pl.when(i == 0)
    def _():
        y_ref[...] = jnp.zeros_like(y_ref)

    @pl.when(i > 0)
    def _():
        slot = (i - 1) % 2
        xe = xe_ref[0]
        acc = jnp.zeros(y_ref.shape[1:], F32)
        for c in range(EXPERT_FF // FF_CHUNK):
            cols = slice(c * FF_CHUNK, (c + 1) * FF_CHUNK)
            a = _dot(xe, w1b[slot, :, cols])
            g = _dot(xe, w3b[slot, :, cols])
            hid = (a / (1.0 + jnp.exp(-a))) * g
            acc = acc + _dot(hid.astype(BF16), w2b[slot, cols, :])
        y_ref[0] = acc


def _ffn(xe, w1, w3, w2, tile):
    e, m, _ = xe.shape
    steps = m // tile
    cur = lambda i, j: (jnp.maximum(i - 1, 0), j, 0)
    nxt = lambda i, j: (jnp.minimum(i, e - 1), j, 0)
    return pl.pallas_call(
        functools.partial(_ffn_kernel, e),
        grid=(e + 1, steps),
        in_specs=[pl.BlockSpec((1, tile, D_MODEL), cur),
                  pl.BlockSpec((1, D_MODEL // steps, EXPERT_FF), nxt),
                  pl.BlockSpec((1, D_MODEL // steps, EXPERT_FF), nxt),
                  pl.BlockSpec((1, EXPERT_FF // steps, D_MODEL), nxt)],
        out_specs=pl.BlockSpec((1, tile, D_MODEL), lambda i, j: (i, j, 0)),
        out_shape=jax.ShapeDtypeStruct((e + 1, m, D_MODEL), F32),
        scratch_shapes=[pltpu.VMEM((2, D_MODEL, EXPERT_FF), BF16),
                        pltpu.VMEM((2, D_MODEL, EXPERT_FF), BF16),
                        pltpu.VMEM((2, EXPERT_FF, D_MODEL), BF16)],
        compiler_params=_params("arbitrary", "arbitrary"),
        name="ffn",
    )(xe, w1, w3, w2)


def _combine_kernel(cap, y_ref, pos_ref, aff_ref, xn_ref, gt_ref, g_ref, o_ref, acc_ref):
    e = pl.program_id(2)

    @pl.when(e == 0)
    def _():
        acc_ref[...] = jnp.zeros_like(acc_ref)

    lane = lax.broadcasted_iota(jnp.int32, pos_ref.shape[1:], 1)
    mine = lane == e
    pos = jnp.sum(jnp.where(mine, pos_ref[0].astype(F32), 0.0), axis=-1, keepdims=True)
    gate = jnp.sum(jnp.where(mine, aff_ref[0], 0.0), axis=-1, keepdims=True)
    slot = lax.broadcasted_iota(jnp.int32, (pos.shape[0], cap), 1).astype(F32)
    onehot = jnp.where(pos == slot, 1.0, 0.0).astype(BF16)
    y = y_ref[0]
    hi = y.astype(BF16)
    lo = (y - hi.astype(F32)).astype(BF16)
    acc_ref[...] += gate * (_dot(onehot, hi) + _dot(onehot, lo))

    @pl.when(e == pl.num_programs(2) - 1)
    def _():
        z = xn_ref[0] + gt_ref[0] * acc_ref[...]
        o_ref[0] = (z * _rms(z)) * g_ref[...]


def _combine(y, pos_n, aff_n, xn, gt2, g_final, cap, tile):
    b, n, _ = xn.shape
    e = y.shape[0] - 1
    row = lambda i, j, k: (i, j, 0)
    return pl.pallas_call(
        functools.partial(_combine_kernel, cap),
        grid=(b, n // tile, e),
        in_specs=[pl.BlockSpec((1, cap, D_MODEL), lambda i, j, k: (k + 1, i, 0)),
                  pl.BlockSpec((1, tile, e), row),
                  pl.BlockSpec((1, tile, e), row),
                  pl.BlockSpec((1, tile, D_MODEL), row),
                  pl.BlockSpec((1, 1, D_MODEL), lambda i, j, k: (i, 0, 0)),
                  pl.BlockSpec((1, D_MODEL), lambda i, j, k: (0, 0))],
        out_specs=pl.BlockSpec((1, tile, D_MODEL), row),
        out_shape=jax.ShapeDtypeStruct((b, n, D_MODEL), F32),
        scratch_shapes=[pltpu.VMEM((tile, D_MODEL), F32)],
        compiler_params=_params("arbitrary", "arbitrary", "arbitrary"),
        name="combine",
    )(y, pos_n, aff_n, xn, gt2, g_final)


def _rope_tables(n):
    pos = jnp.arange(n)
    n_freq = HEAD_DIM // 4
    inv = ROPE_THETA ** (-jnp.arange(n_freq, dtype=F32) / n_freq)
    ang = jnp.concatenate([(pos // GRID_W)[:, None].astype(F32) * inv,
                           (pos % GRID_W)[:, None].astype(F32) * inv], axis=-1)
    cos, sin = jnp.cos(ang), jnp.sin(ang)
    cos_t = jnp.tile(jnp.concatenate([cos, cos], axis=-1), (1, N_Q_HEADS))
    sin_t = jnp.tile(jnp.concatenate([-sin, sin], axis=-1), (1, N_Q_HEADS))
    return cos_t, sin_t


def _block_diag_ones(width, block):
    i = jnp.arange(width) // block
    return (i[:, None] == i[None, :]).astype(BF16)


def kernel(x, c, ctx, c_ctx, w_mod, b_mod, g_mix, g_ffn, w_in, q_gain, k_gain, v_gain, w_s, b_s,
           w_out, w_router, w1, w3, w2, g_final):
    assert w_mod.shape[0] == 1, "single-layer kernel"
    b, n, d = x.shape
    cap = CAPACITY_FACTOR * n // N_EXPERTS

    rows = -(-(b + 1) // 8) * 8
    cvecs = jnp.zeros((rows, d), F32).at[:b].set(c).at[b].set(c_ctx)
    mods = _adaln(cvecs, w_mod[0], b_mod[0])
    sh1, sc1, gt1, sh2, sc2, gt2 = [m[:b, None, :] for m in jnp.split(mods, 6, axis=-1)]
    csh1, csc1 = mods[b:b + 1, :d], mods[b:b + 1, d:2 * d]

    cos_t, sin_t = _rope_tables(n)
    bd = _block_diag_ones(256, HEAD_DIM)
    w_in_b = w_in[0].astype(BF16)
    g_mix2 = g_mix[0][None, :]
    qg = jnp.tile(q_gain[0], N_Q_HEADS)[None, :]
    kg = jnp.tile(k_gain[0], N_KV_HEADS)[None, :]

    q, kl, vl, u, vvn = _inproj(x, sh1, sc1, g_mix2, w_in_b, bd, qg, kg, v_gain[0][None, :],
                                cos_t, sin_t, tile=256)
    kc, vc = _ctxkv(ctx, csh1, csc1, g_mix2, w_in_b, bd, kg)
    vt_all = jnp.swapaxes(jnp.concatenate([vl, vc], axis=1), 1, 2)
    o = _attn(q, jnp.concatenate([kl, kc], axis=1), vt_all, tile=256)

    bs = jnp.repeat(b_s[0].T, GM_GROUP_DIM, axis=1)
    xn, h2, aff_t = _mixout(o, u, vvn, x, w_s[0].astype(BF16), bs, w_out[0].astype(BF16), gt1,
                            g_ffn[0][None, :], sh2, sc2, w_router[0].T.astype(BF16), tile=256)

    tri = (jnp.arange(128)[:, None] <= jnp.arange(128)[None, :]).astype(BF16)
    pos_t = _route(aff_t, tri, cap, rows=64)
    xe = _gather(pos_t, h2, cap)
    y = _ffn(xe, w1[0], w3[0], w2[0], tile=512)
    return _combine(y, jnp.swapaxes(pos_t, 1, 2), jnp.swapaxes(aff_t, 1, 2), xn, gt2,
                    g_final[None, :], cap, tile=1024)
```

```python
import functools
import math

import jax
import jax.numpy as jnp
from jax import lax
from jax.experimental import pallas as pl
from jax.experimental.pallas import tpu as pltpu

D_MODEL = 1024
CTX_LEN = 256
GRID_W = 64
HEAD_DIM = 64
N_Q_HEADS = 8
N_KV_HEADS = 2
Q_PER_KV = N_Q_HEADS // N_KV_HEADS
D_ATTN = N_Q_HEADS * HEAD_DIM
D_KV = N_KV_HEADS * HEAD_DIM
N_GM_GROUPS = 8
GM_GROUP_DIM = 64
D_GM = N_GM_GROUPS * GM_GROUP_DIM
CHUNK = 128
D_IN = D_ATTN + 2 * D_KV + 2 * D_GM
ROPE_THETA = 10000.0
N_EXPERTS = 16
CAPACITY_FACTOR = 2
EXPERT_FF = 2048
EPS = 1e-6

F32 = jnp.float32
BF16 = jnp.bfloat16

VMEM_LIMIT_BYTES = 52 * 1024 * 1024
LANES = 128
SLABS = D_MODEL // LANES

K_OFF = D_ATTN
V_OFF = D_ATTN + D_KV
GM_OFF = D_ATTN + 2 * D_KV

Q_SCALE = math.log2(math.e) * HEAD_DIM ** -0.5


def _params(*sem):
    return pltpu.CompilerParams(dimension_semantics=sem, vmem_limit_bytes=VMEM_LIMIT_BYTES)


def _dot(a, b):
    return jnp.dot(a, b, preferred_element_type=F32)


def _dot_nt(a, b):
    return lax.dot_general(a, b, (((1,), (1,)), ((), ())), preferred_element_type=F32)


def _rms(v):
    return lax.rsqrt(jnp.sum(v * v, axis=-1, keepdims=True) * (1.0 / v.shape[-1]) + EPS)


def _modulate(x, g, shift, scale):
    return ((x * _rms(x)) * g) * (1.0 + scale) + shift


def _seg_sum(v, bd):
    w = bd.shape[0]
    outs = []
    for j in range(v.shape[1] // w):
        c = v[:, j * w:(j + 1) * w]
        hi = c.astype(BF16)
        lo = (c - hi.astype(F32)).astype(BF16)
        outs.append(_dot(hi, bd) + _dot(lo, bd))
    return outs[0] if len(outs) == 1 else jnp.concatenate(outs, axis=-1)


def _head_norm(z, bd, gain):
    ss = _seg_sum(z * z, bd)
    return (z * lax.rsqrt(ss * (1.0 / HEAD_DIM) + EPS)) * gain


def _rope(v, cos, sin_signed):
    w = v.shape[-1]
    lane = lax.broadcasted_iota(jnp.int32, v.shape, 1)
    partner = jnp.where((lane & (HEAD_DIM // 2)) == 0,
                        pltpu.roll(v, w - HEAD_DIM // 2, axis=1),
                        pltpu.roll(v, HEAD_DIM // 2, axis=1))
    return v * cos + partner * sin_signed


def _gelu(z):
    return 0.5 * z * (1.0 + lax.erf(z * math.sqrt(0.5)))


def _adaln_kernel(c_ref, w_ref, b_ref, o_ref):
    c = c_ref[...]
    s = c / (1.0 + jnp.exp(-c))
    o_ref[...] = _dot(s.astype(BF16), w_ref[...].astype(BF16)) + b_ref[...]


def _adaln(cvecs, w_mod, b_mod):
    rows = cvecs.shape[0]
    n_out = w_mod.shape[1]
    tn = 768
    return pl.pallas_call(
        _adaln_kernel,
        grid=(n_out // tn,),
        in_specs=[pl.BlockSpec((rows, D_MODEL), lambda j: (0, 0)),
                  pl.BlockSpec((D_MODEL, tn), lambda j: (0, j)),
                  pl.BlockSpec((1, tn), lambda j: (0, j))],
        out_specs=pl.BlockSpec((rows, tn), lambda j: (0, j)),
        out_shape=jax.ShapeDtypeStruct((rows, n_out), F32),
        compiler_params=_params("arbitrary"),
        name="adaln",
    )(cvecs, w_mod, b_mod.reshape(1, n_out))


def _inproj_kernel(x_ref, sh_ref, sc_ref, g_ref, w_ref, bd_ref, qg_ref, kg_ref, vg_ref,
                   cos_ref, sin_ref, q_ref, k_ref, v_ref, u_ref, vv_ref):
    h = _modulate(x_ref[0], g_ref[...], sh_ref[0], sc_ref[0])
    z = _dot(h.astype(BF16), w_ref[...])
    cos = cos_ref[...]
    sin = sin_ref[...]
    bd = bd_ref[...]

    q = _rope(_head_norm(z[:, :D_ATTN], bd, qg_ref[...]), cos, sin)
    q_ref[0] = (q * Q_SCALE).astype(BF16)

    k = _head_norm(z[:, K_OFF:K_OFF + D_KV], bd[:D_KV, :D_KV], kg_ref[...])
    k_ref[0] = _rope(k, cos[:, :D_KV], sin[:, :D_KV]).astype(BF16)
    v_ref[0] = z[:, V_OFF:V_OFF + D_KV].astype(BF16)

    gz = _gelu(z[:, GM_OFF:])
    u_ref[0] = gz[:, :D_GM]
    vv = gz[:, D_GM:]
    vv_ref[0] = ((vv * _rms(vv)) * vg_ref[...]).astype(BF16)


def _inproj(x, sh1, sc1, g_mix, w_in, bd, qg, kg, vg, cos_t, sin_t, tile):
    b, n, _ = x.shape
    row = lambda i, j: (j, i, 0)
    per_b = lambda i, j: (j, 0, 0)
    const = lambda i, j: (0, 0)
    tab = lambda i, j: (i, 0)
    return pl.pallas_call(
        _inproj_kernel,
        grid=(n // tile, b),
        in_specs=[pl.BlockSpec((1, tile, D_MODEL), row),
                  pl.BlockSpec((1, 1, D_MODEL), per_b),
                  pl.BlockSpec((1, 1, D_MODEL), per_b),
                  pl.BlockSpec((1, D_MODEL), const),
                  pl.BlockSpec((D_MODEL, D_IN), const),
                  pl.BlockSpec(bd.shape, const),
                  pl.BlockSpec((1, D_ATTN), const),
                  pl.BlockSpec((1, D_KV), const),
                  pl.BlockSpec((1, D_GM), const),
                  pl.BlockSpec((tile, D_ATTN), tab),
                  pl.BlockSpec((tile, D_ATTN), tab)],
        out_specs=[pl.BlockSpec((1, tile, D_ATTN), row),
                   pl.BlockSpec((1, tile, D_KV), row),
                   pl.BlockSpec((1, tile, D_KV), row),
                   pl.BlockSpec((1, tile, D_GM), row),
                   pl.BlockSpec((1, tile, D_GM), row)],
        out_shape=[jax.ShapeDtypeStruct((b, n, D_ATTN), BF16),
                   jax.ShapeDtypeStruct((b, n, D_KV), BF16),
                   jax.ShapeDtypeStruct((b, n, D_KV), BF16),
                   jax.ShapeDtypeStruct((b, n, D_GM), F32),
                   jax.ShapeDtypeStruct((b, n, D_GM), BF16)],
        compiler_params=_params("arbitrary", "arbitrary"),
        name="inproj",
    )(x, sh1, sc1, g_mix, w_in, bd, qg, kg, vg, cos_t, sin_t)


def _ctxkv_kernel(x_ref, sh_ref, sc_ref, g_ref, w_ref, bd_ref, kg_ref, k_ref, v_ref):
    h = _modulate(x_ref[0], g_ref[...], sh_ref[...], sc_ref[...])
    z = _dot(h.astype(BF16), w_ref[:, K_OFF:K_OFF + 2 * D_KV])
    k_ref[0] = _head_norm(z[:, :D_KV], bd_ref[:D_KV, :D_KV], kg_ref[...]).astype(BF16)
    v_ref[0] = z[:, D_KV:].astype(BF16)


def _ctxkv(ctx, csh1, csc1, g_mix, w_in, bd, kg):
    b, n, _ = ctx.shape
    const = lambda i: (0, 0)
    row = lambda i: (i, 0, 0)
    return pl.pallas_call(
        _ctxkv_kernel,
        grid=(b,),
        in_specs=[pl.BlockSpec((1, n, D_MODEL), row),
                  pl.BlockSpec((1, D_MODEL), const),
                  pl.BlockSpec((1, D_MODEL), const),
                  pl.BlockSpec((1, D_MODEL), const),
                  pl.BlockSpec((D_MODEL, D_IN), const),
                  pl.BlockSpec(bd.shape, const),
                  pl.BlockSpec((1, D_KV), const)],
        out_specs=[pl.BlockSpec((1, n, D_KV), row),
                   pl.BlockSpec((1, n, D_KV), row)],
        out_shape=[jax.ShapeDtypeStruct((b, n, D_KV), BF16),
                   jax.ShapeDtypeStruct((b, n, D_KV), BF16)],
        compiler_params=_params("arbitrary"),
        name="ctxkv",
    )(ctx, csh1, csc1, g_mix, w_in, bd, kg)


ATTN_LOOKAHEAD = 4


def _attn_kernel(q_ref, k_ref, vt_ref, o_ref, st_ref):
    q = q_ref[0]
    k = k_ref[0]
    vt = vt_ref[0]
    def scores(h):
        kv = h // Q_PER_KV
        qh = q[:, h * HEAD_DIM:(h + 1) * HEAD_DIM]
        kh = k[:, kv * HEAD_DIM:(kv + 1) * HEAD_DIM]
        st_ref[h] = _dot_nt(kh, qh)

    for h in range(ATTN_LOOKAHEAD):
        scores(h)
    outs = []
    for h in range(N_Q_HEADS):
        if h + ATTN_LOOKAHEAD < N_Q_HEADS:
            scores(h + ATTN_LOOKAHEAD)
        kv = h // Q_PER_KV
        st = st_ref[h]
        pt = jnp.exp2(st - jnp.max(st, axis=0, keepdims=True))
        l = jnp.sum(pt, axis=0, keepdims=True)
        outs.append(_dot(vt[kv * HEAD_DIM:(kv + 1) * HEAD_DIM, :], pt.astype(BF16)) / l)
    o_ref[0] = jnp.concatenate(outs, axis=0).T.astype(BF16)


def _attn(q, k_all, vt_all, tile):
    b, n, _ = q.shape
    nk = k_all.shape[1]
    return pl.pallas_call(
        _attn_kernel,
        grid=(b, n // tile),
        in_specs=[pl.BlockSpec((1, tile, D_ATTN), lambda i, j: (i, j, 0)),
                  pl.BlockSpec((1, nk, D_KV), lambda i, j: (i, 0, 0)),
                  pl.BlockSpec((1, D_KV, nk), lambda i, j: (i, 0, 0))],
        out_specs=pl.BlockSpec((1, tile, D_ATTN), lambda i, j: (i, j, 0)),
        out_shape=jax.ShapeDtypeStruct((b, n, D_ATTN), BF16),
        scratch_shapes=[pltpu.VMEM((N_Q_HEADS, nk, tile), F32)],
        compiler_params=_params("arbitrary", "arbitrary"),
        name="attn",
    )(q, k_all, vt_all)


def _mixout_kernel(o_ref, u_ref, vv_ref, x_ref, ws_ref, bs_ref, wout_ref, gt_ref, g_ref,
                   sh_ref, sc_ref, wr_ref, xn_ref, h_ref, aff_ref):
    tile = o_ref.shape[1]
    seg = lax.broadcasted_iota(jnp.int32, (CHUNK, D_GM), 1) // GM_GROUP_DIM
    bs = bs_ref[...]
    gms = []
    for c in range(tile // CHUNK):
        vc = vv_ref[0, c * CHUNK:(c + 1) * CHUNK, :]
        mixed = jnp.zeros((CHUNK, D_GM), F32)
        for g in range(N_GM_GROUPS):
            mixed = jnp.where(seg == g, _dot(ws_ref[g], vc), mixed)
        gms.append(u_ref[0, c * CHUNK:(c + 1) * CHUNK, :] * (mixed + bs))
    gm = jnp.concatenate(gms, axis=0).astype(BF16)

    proj = _dot(o_ref[0], wout_ref[:D_ATTN, :]) + _dot(gm, wout_ref[D_ATTN:, :])
    xn = x_ref[0] + gt_ref[0] * proj
    xn_ref[0] = xn

    h = _modulate(xn, g_ref[...], sh_ref[0], sc_ref[0]).astype(BF16)
    h_ref[0] = h
    logits = _dot_nt(wr_ref[...], h)
    e = jnp.exp(logits - jnp.max(logits, axis=0, keepdims=True))
    aff_ref[0] = e / jnp.sum(e, axis=0, keepdims=True)


def _mixout(o, u, vvn, x, w_s, bs, w_out, gt1, g_ffn, sh2, sc2, w_rt, tile):
    b, n, _ = x.shape
    row = lambda i, j: (i, j, 0)
    per_b = lambda i, j: (i, 0, 0)
    const2 = lambda i, j: (0, 0)
    return pl.pallas_call(
        _mixout_kernel,
        grid=(b, n // tile),
        in_specs=[pl.BlockSpec((1, tile, D_ATTN), row),
                  pl.BlockSpec((1, tile, D_GM), row),
                  pl.BlockSpec((1, tile, D_GM), row),
                  pl.BlockSpec((1, tile, D_MODEL), row),
                  pl.BlockSpec((N_GM_GROUPS, CHUNK, CHUNK), lambda i, j: (0, 0, 0)),
                  pl.BlockSpec((CHUNK, D_GM), const2),
                  pl.BlockSpec((D_ATTN + D_GM, D_MODEL), const2),
                  pl.BlockSpec((1, 1, D_MODEL), per_b),
                  pl.BlockSpec((1, D_MODEL), const2),
                  pl.BlockSpec((1, 1, D_MODEL), per_b),
                  pl.BlockSpec((1, 1, D_MODEL), per_b),
                  pl.BlockSpec((N_EXPERTS, D_MODEL), const2)],
        out_specs=[pl.BlockSpec((1, tile, D_MODEL), row),
                   pl.BlockSpec((1, tile, D_MODEL), row),
                   pl.BlockSpec((1, N_EXPERTS, tile), lambda i, j: (i, 0, j))],
        out_shape=[jax.ShapeDtypeStruct((b, n, D_MODEL), F32),
                   jax.ShapeDtypeStruct((b, n, D_MODEL), BF16),
                   jax.ShapeDtypeStruct((b, N_EXPERTS, n), F32)],
        compiler_params=_params("arbitrary", "arbitrary"),
        name="mixout",
    )(o, u, vvn, x, w_s, bs, w_out, gt1, g_ffn, sh2, sc2, w_rt)


def _cumsum_lanes(m, tri):
    off = jnp.zeros((m.shape[0], 1), F32)
    outs = []
    for blk in range(m.shape[1] // 128):
        mb = m[:, blk * 128:(blk + 1) * 128]
        outs.append(_dot(mb.astype(BF16), tri) + off)
        off = off + jnp.sum(mb, axis=-1, keepdims=True)
    return jnp.concatenate(outs, axis=-1)


REFINE_STEPS = 32


def _route_kernel(cap, aff_ref, tri_ref, pos_ref):
    a = aff_ref[...]
    capf = float(cap)

    def count_ge(t):
        return jnp.sum(jnp.where(a >= t, 1.0, 0.0), axis=-1, keepdims=True)

    def bit_step(i, prefix):
        cand = prefix | jnp.left_shift(jnp.int32(1), 30 - i)
        return jnp.where(count_ge(pltpu.bitcast(cand, F32)) >= capf, cand, prefix)

    prefix = lax.fori_loop(0, 31, bit_step, jnp.zeros((a.shape[0], 1), jnp.int32))
    lo = pltpu.bitcast(prefix, F32)
    hi = pltpu.bitcast(prefix + 1, F32)

    def refine(_, carry):
        lo, hi = carry
        mid = 0.5 * lo + 0.5 * hi
        ok = count_ge(mid) >= capf
        return jnp.where(ok, mid, lo), jnp.where(ok, hi, mid)

    lo, hi = lax.fori_loop(0, REFINE_STEPS, refine, (lo, hi))
    thr = jnp.min(jnp.where(a >= lo, a, jnp.inf), axis=-1, keepdims=True)

    gt = jnp.where(a > thr, 1.0, 0.0)
    eq = jnp.where(a == thr, 1.0, 0.0)
    need = capf - jnp.sum(gt, axis=-1, keepdims=True)
    tri = tri_ref[...]
    sel = gt + eq * jnp.where(_cumsum_lanes(eq, tri) <= need, 1.0, 0.0)
    slot = _cumsum_lanes(sel, tri) - 1.0
    pos_ref[...] = jnp.where(sel > 0.0, slot, -1.0).astype(jnp.int32)


def _route(aff_t, tri, cap, rows):
    b, e, n = aff_t.shape
    pos = pl.pallas_call(
        functools.partial(_route_kernel, cap),
        grid=(b * e // rows,),
        in_specs=[pl.BlockSpec((rows, n), lambda i: (i, 0)),
                  pl.BlockSpec((128, 128), lambda i: (0, 0))],
        out_specs=pl.BlockSpec((rows, n), lambda i: (i, 0)),
        out_shape=jax.ShapeDtypeStruct((b * e, n), jnp.int32),
        compiler_params=_params("arbitrary"),
        name="route",
    )(aff_t.reshape(b * e, n), tri)
    return pos.reshape(b, e, n)


def _gather_kernel(cap, pos_ref, aff_ref, h_ref, xe_ref, gate_ref, tok_ref):
    pos = pos_ref[0, 0]
    n = pos.shape[-1]
    hit = pos == lax.broadcasted_iota(jnp.int32, (cap, n), 0)
    xe_ref[0] = _dot(jnp.where(hit, 1.0, 0.0).astype(BF16), h_ref[0]).astype(BF16)
    gate_ref[0] = jnp.sum(jnp.where(hit, aff_ref[0, 0], 0.0), axis=-1, keepdims=True)
    tok = lax.broadcasted_iota(jnp.int32, (cap, n), 1).astype(F32)
    tok_ref[0] = jnp.sum(jnp.where(hit, tok, 0.0), axis=-1, keepdims=True).astype(jnp.int32)


def _gather(pos_t, aff_t, h2, cap):
    b, e, n = pos_t.shape
    row = lambda i, j: (i, j, 0, 0)
    slots = lambda i, j: (j, i, 0)
    return pl.pallas_call(
        functools.partial(_gather_kernel, cap),
        grid=(b, e),
        in_specs=[pl.BlockSpec((1, 1, 1, n), row),
                  pl.BlockSpec((1, 1, 1, n), row),
                  pl.BlockSpec((1, n, D_MODEL), lambda i, j: (i, 0, 0))],
        out_specs=[pl.BlockSpec((1, cap, D_MODEL), slots),
                   pl.BlockSpec((1, cap, 1), slots),
                   pl.BlockSpec((1, cap, 1), slots)],
        out_shape=[jax.ShapeDtypeStruct((e, b * cap, D_MODEL), BF16),
                   jax.ShapeDtypeStruct((e, b * cap, 1), F32),
                   jax.ShapeDtypeStruct((e, b * cap, 1), jnp.int32)],
        compiler_params=_params("arbitrary", "arbitrary"),
        name="gather",
    )(pos_t.reshape(b, e, 1, n), aff_t.reshape(b, e, 1, n), h2)


FF_CHUNK = 512


def _ffn_kernel(n_experts, xe_ref, gate_ref, w1_ref, w3_ref, w2_ref, y_ref, w1b, w3b, w2b):
    i = pl.program_id(0)
    j = pl.program_id(1)
    rows13 = w1_ref.shape[1]
    rows2 = w2_ref.shape[1]
    tile = xe_ref.shape[1]

    @pl.when(i < n_experts)
    def _():
        slot = i % 2
        w1b[slot, pl.ds(j * rows13, rows13), :] = w1_ref[0].astype(BF16)
        w3b[slot, pl.ds(j * rows13, rows13), :] = w3_ref[0].astype(BF16)
        w2b[slot, pl.ds(j * rows2, rows2), :] = w2_ref[0].astype(BF16)

    @pl.when(i == 0)
    def _():
        y_ref[...] = jnp.zeros_like(y_ref)

    @pl.when(i > 0)
    def _():
        slot = (i - 1) % 2
        xe = xe_ref[0]
        acc = jnp.zeros((tile, D_MODEL), F32)
        for c in range(EXPERT_FF // FF_CHUNK):
            cols = slice(c * FF_CHUNK, (c + 1) * FF_CHUNK)
            a = _dot(xe, w1b[slot, :, cols])
            g = _dot(xe, w3b[slot, :, cols])
            hid = (a / (1.0 + jnp.exp(-a))) * g
            acc = acc + _dot(hid.astype(BF16), w2b[slot, cols, :])
        y = acc * gate_ref[0]
        for s in range(SLABS):
            y_ref[0, pl.ds(s, tile, stride=SLABS), :] = y[:, s * LANES:(s + 1) * LANES]


def _ffn(xe, gate, w1, w3, w2, tile):
    e, m, _ = xe.shape
    steps = m // tile
    cur = lambda i, j: (jnp.maximum(i - 1, 0), j, 0)
    nxt = lambda i, j: (jnp.minimum(i, e - 1), j, 0)
    return pl.pallas_call(
        functools.partial(_ffn_kernel, e),
        grid=(e + 1, steps),
        in_specs=[pl.BlockSpec((1, tile, D_MODEL), cur),
                  pl.BlockSpec((1, tile, 1), cur),
                  pl.BlockSpec((1, D_MODEL // steps, EXPERT_FF), nxt),
                  pl.BlockSpec((1, D_MODEL // steps, EXPERT_FF), nxt),
                  pl.BlockSpec((1, EXPERT_FF // steps, D_MODEL), nxt)],
        out_specs=pl.BlockSpec((1, tile * SLABS, LANES), lambda i, j: (i, j, 0)),
        out_shape=jax.ShapeDtypeStruct((e + 1, m * SLABS, LANES), F32),
        scratch_shapes=[pltpu.VMEM((2, D_MODEL, EXPERT_FF), BF16),
                        pltpu.VMEM((2, D_MODEL, EXPERT_FF), BF16),
                        pltpu.VMEM((2, EXPERT_FF, D_MODEL), BF16)],
        compiler_params=_params("arbitrary", "arbitrary"),
        name="ffn",
    )(xe, gate, w1, w3, w2)


SCATTER_UNROLL = 8
FINAL_TILE = 256


def _combine_kernel(cap, n_batch, tok_ref, y_ref, xn_ref, gt_ref, g_ref, o_ref, acc_ref):
    b = pl.program_id(0)
    e = pl.program_id(1)

    @pl.when(e == 0)
    def _():
        acc_ref[...] = jnp.zeros_like(acc_ref)

    base = (e * n_batch + b) * cap

    def tile_at(i):
        return pl.ds(pl.multiple_of(i * SLABS, SLABS), SLABS)

    def group(gi, carry):
        rows, vals = [], []
        for u in range(SCATTER_UNROLL):
            r = gi * SCATTER_UNROLL + u
            row = tile_at(tok_ref[base + r])
            rows.append(row)
            vals.append(acc_ref[row, :] + y_ref[0, tile_at(r), :])
        for row, val in zip(rows, vals):
            acc_ref[row, :] = val
        return carry

    lax.fori_loop(0, cap // SCATTER_UNROLL, group, 0)

    @pl.when(e == pl.num_programs(1) - 1)
    def _():
        for i in range(xn_ref.shape[1] // FINAL_TILE):
            rows = slice(i * FINAL_TILE, (i + 1) * FINAL_TILE)
            moe = jnp.concatenate(
                [acc_ref[pl.ds(i * FINAL_TILE * SLABS + s, FINAL_TILE, stride=SLABS), :]
                 for s in range(SLABS)], axis=-1)
            z = xn_ref[0, rows, :] + gt_ref[0] * moe
            o_ref[0, rows, :] = (z * _rms(z)) * g_ref[...]


def _combine(y, tok, xn, gt2, g_final, cap):
    b, n, _ = xn.shape
    e = y.shape[0] - 1
    per_b = lambda i, k, tok: (i, 0, 0)
    return pl.pallas_call(
        functools.partial(_combine_kernel, cap, b),
        grid_spec=pltpu.PrefetchScalarGridSpec(
            num_scalar_prefetch=1,
            grid=(b, e),
            in_specs=[pl.BlockSpec((1, cap * SLABS, LANES), lambda i, k, tok: (k + 1, i, 0)),
                      pl.BlockSpec((1, n, D_MODEL), per_b),
                      pl.BlockSpec((1, 1, D_MODEL), per_b),
                      pl.BlockSpec((1, D_MODEL), lambda i, k, tok: (0, 0))],
            out_specs=pl.BlockSpec((1, n, D_MODEL), per_b),
            scratch_shapes=[pltpu.VMEM((n * SLABS, LANES), F32)]),
        out_shape=jax.ShapeDtypeStruct((b, n, D_MODEL), F32),
        compiler_params=_params("arbitrary", "arbitrary"),
        name="combine",
    )(tok, y, xn, gt2, g_final)


def _rope_tables(n):
    pos = jnp.arange(n)
    n_freq = HEAD_DIM // 4
    inv = ROPE_THETA ** (-jnp.arange(n_freq, dtype=F32) / n_freq)
    ang = jnp.concatenate([(pos // GRID_W)[:, None].astype(F32) * inv,
                           (pos % GRID_W)[:, None].astype(F32) * inv], axis=-1)
    cos, sin = jnp.cos(ang), jnp.sin(ang)
    cos_t = jnp.tile(jnp.concatenate([cos, cos], axis=-1), (1, N_Q_HEADS))
    sin_t = jnp.tile(jnp.concatenate([-sin, sin], axis=-1), (1, N_Q_HEADS))
    return cos_t, sin_t


def _block_diag_ones(width, block):
    i = jnp.arange(width) // block
    return (i[:, None] == i[None, :]).astype(BF16)


def kernel(x, c, ctx, c_ctx, w_mod, b_mod, g_mix, g_ffn, w_in, q_gain, k_gain, v_gain, w_s, b_s,
           w_out, w_router, w1, w3, w2, g_final):
    assert w_mod.shape[0] == 1, "single-layer kernel"
    b, n, d = x.shape
    cap = CAPACITY_FACTOR * n // N_EXPERTS

    rows = -(-(b + 1) // 8) * 8
    cvecs = jnp.zeros((rows, d), F32).at[:b].set(c).at[b].set(c_ctx)
    mods = _adaln(cvecs, w_mod[0], b_mod[0])
    sh1, sc1, gt1, sh2, sc2, gt2 = [m[:b, None, :] for m in jnp.split(mods, 6, axis=-1)]
    csh1, csc1 = mods[b:b + 1, :d], mods[b:b + 1, d:2 * d]

    cos_t, sin_t = _rope_tables(n)
    bd = _block_diag_ones(256, HEAD_DIM)
    w_in_b = w_in[0].astype(BF16)
    g_mix2 = g_mix[0][None, :]
    qg = jnp.tile(q_gain[0], N_Q_HEADS)[None, :]
    kg = jnp.tile(k_gain[0], N_KV_HEADS)[None, :]

    q, kl, vl, u, vvn = _inproj(x, sh1, sc1, g_mix2, w_in_b, bd, qg, kg, v_gain[0][None, :],
                                cos_t, sin_t, tile=256)
    kc, vc = _ctxkv(ctx, csh1, csc1, g_mix2, w_in_b, bd, kg)
    vt_all = jnp.swapaxes(jnp.concatenate([vl, vc], axis=1), 1, 2)
    o = _attn(q, jnp.concatenate([kl, kc], axis=1), vt_all, tile=256)

    bs = jnp.repeat(b_s[0].T, GM_GROUP_DIM, axis=1)
    xn, h2, aff_t = _mixout(o, u, vvn, x, w_s[0].astype(BF16), bs, w_out[0].astype(BF16), gt1,
                            g_ffn[0][None, :], sh2, sc2, w_router[0].T.astype(BF16), tile=256)

    tri = (jnp.arange(128)[:, None] <= jnp.arange(128)[None, :]).astype(BF16)
    pos_t = _route(aff_t, tri, cap, rows=64)
    xe, gate, tok = _gather(pos_t, aff_t, h2, cap)
    y = _ffn(xe, gate, w1[0], w3[0], w2[0], tile=512)
    return _combine(y, tok.reshape(-1), xn, gt2, g_final[None, :], cap)
```

```python
import functools
import math

import jax
import jax.numpy as jnp
from jax import lax
from jax.experimental import pallas as pl
from jax.experimental.pallas import tpu as pltpu

D_MODEL = 1024
CTX_LEN = 256
GRID_W = 64
HEAD_DIM = 64
N_Q_HEADS = 8
N_KV_HEADS = 2
Q_PER_KV = N_Q_HEADS // N_KV_HEADS
D_ATTN = N_Q_HEADS * HEAD_DIM
D_KV = N_KV_HEADS * HEAD_DIM
N_GM_GROUPS = 8
GM_GROUP_DIM = 64
D_GM = N_GM_GROUPS * GM_GROUP_DIM
CHUNK = 128
D_IN = D_ATTN + 2 * D_KV + 2 * D_GM
ROPE_THETA = 10000.0
N_EXPERTS = 16
CAPACITY_FACTOR = 2
EXPERT_FF = 2048
EPS = 1e-6

F32 = jnp.float32
BF16 = jnp.bfloat16

VMEM_LIMIT_BYTES = 52 * 1024 * 1024
LANES = 128
SLABS = D_MODEL // LANES

K_OFF = D_ATTN
V_OFF = D_ATTN + D_KV
GM_OFF = D_ATTN + 2 * D_KV

Q_SCALE = math.log2(math.e) * HEAD_DIM ** -0.5


def _params(*sem):
    return pltpu.CompilerParams(dimension_semantics=sem, vmem_limit_bytes=VMEM_LIMIT_BYTES)


def _dot(a, b):
    return jnp.dot(a, b, preferred_element_type=F32)


def _dot_nt(a, b):
    return lax.dot_general(a, b, (((1,), (1,)), ((), ())), preferred_element_type=F32)


def _rms(v):
    return lax.rsqrt(jnp.sum(v * v, axis=-1, keepdims=True) * (1.0 / v.shape[-1]) + EPS)


def _modulate(x, g, shift, scale):
    return ((x * _rms(x)) * g) * (1.0 + scale) + shift


def _seg_sum(v, bd):
    w = bd.shape[0]
    outs = []
    for j in range(v.shape[1] // w):
        c = v[:, j * w:(j + 1) * w]
        hi = c.astype(BF16)
        lo = (c - hi.astype(F32)).astype(BF16)
        outs.append(_dot(hi, bd) + _dot(lo, bd))
    return outs[0] if len(outs) == 1 else jnp.concatenate(outs, axis=-1)


def _head_norm(z, bd, gain):
    ss = _seg_sum(z * z, bd)
    return (z * lax.rsqrt(ss * (1.0 / HEAD_DIM) + EPS)) * gain


def _rope(v, cos, sin_signed):
    w = v.shape[-1]
    lane = lax.broadcasted_iota(jnp.int32, v.shape, 1)
    partner = jnp.where((lane & (HEAD_DIM // 2)) == 0,
                        pltpu.roll(v, w - HEAD_DIM // 2, axis=1),
                        pltpu.roll(v, HEAD_DIM // 2, axis=1))
    return v * cos + partner * sin_signed


def _gelu(z):
    return 0.5 * z * (1.0 + lax.erf(z * math.sqrt(0.5)))


def _adaln_kernel(c_ref, w_ref, b_ref, o_ref):
    c = c_ref[...]
    s = c / (1.0 + jnp.exp(-c))
    o_ref[...] = _dot(s.astype(BF16), w_ref[...].astype(BF16)) + b_ref[...]


def _adaln(cvecs, w_mod, b_mod):
    rows = cvecs.shape[0]
    n_out = w_mod.shape[1]
    tn = 768
    return pl.pallas_call(
        _adaln_kernel,
        grid=(n_out // tn,),
        in_specs=[pl.BlockSpec((rows, D_MODEL), lambda j: (0, 0)),
                  pl.BlockSpec((D_MODEL, tn), lambda j: (0, j)),
                  pl.BlockSpec((1, tn), lambda j: (0, j))],
        out_specs=pl.BlockSpec((rows, tn), lambda j: (0, j)),
        out_shape=jax.ShapeDtypeStruct((rows, n_out), F32),
        compiler_params=_params("arbitrary"),
        name="adaln",
    )(cvecs, w_mod, b_mod.reshape(1, n_out))


def _inproj_kernel(x_ref, sh_ref, sc_ref, g_ref, w_ref, bd_ref, qg_ref, kg_ref, vg_ref,
                   cos_ref, sin_ref, q_ref, k_ref, v_ref, u_ref, vv_ref):
    h = _modulate(x_ref[0], g_ref[...], sh_ref[0], sc_ref[0])
    z = _dot(h.astype(BF16), w_ref[...])
    cos = cos_ref[...]
    sin = sin_ref[...]
    bd = bd_ref[...]

    q = _rope(_head_norm(z[:, :D_ATTN], bd, qg_ref[...]), cos, sin)
    q_ref[0] = (q * Q_SCALE).astype(BF16)

    k = _head_norm(z[:, K_OFF:K_OFF + D_KV], bd[:D_KV, :D_KV], kg_ref[...])
    k_ref[0] = _rope(k, cos[:, :D_KV], sin[:, :D_KV]).astype(BF16)
    v_ref[0] = z[:, V_OFF:V_OFF + D_KV].astype(BF16)

    gz = _gelu(z[:, GM_OFF:])
    u_ref[0] = gz[:, :D_GM]
    vv = gz[:, D_GM:]
    vv_ref[0] = ((vv * _rms(vv)) * vg_ref[...]).astype(BF16)


def _inproj(x, sh1, sc1, g_mix, w_in, bd, qg, kg, vg, cos_t, sin_t, tile):
    b, n, _ = x.shape
    row = lambda i, j: (j, i, 0)
    per_b = lambda i, j: (j, 0, 0)
    const = lambda i, j: (0, 0)
    tab = lambda i, j: (i, 0)
    return pl.pallas_call(
        _inproj_kernel,
        grid=(n // tile, b),
        in_specs=[pl.BlockSpec((1, tile, D_MODEL), row),
                  pl.BlockSpec((1, 1, D_MODEL), per_b),
                  pl.BlockSpec((1, 1, D_MODEL), per_b),
                  pl.BlockSpec((1, D_MODEL), const),
                  pl.BlockSpec((D_MODEL, D_IN), const),
                  pl.BlockSpec(bd.shape, const),
                  pl.BlockSpec((1, D_ATTN), const),
                  pl.BlockSpec((1, D_KV), const),
                  pl.BlockSpec((1, D_GM), const),
                  pl.BlockSpec((tile, D_ATTN), tab),
                  pl.BlockSpec((tile, D_ATTN), tab)],
        out_specs=[pl.BlockSpec((1, tile, D_ATTN), row),
                   pl.BlockSpec((1, tile, D_KV), row),
                   pl.BlockSpec((1, tile, D_KV), row),
                   pl.BlockSpec((1, tile, D_GM), row),
                   pl.BlockSpec((1, tile, D_GM), row)],
        out_shape=[jax.ShapeDtypeStruct((b, n, D_ATTN), BF16),
                   jax.ShapeDtypeStruct((b, n, D_KV), BF16),
                   jax.ShapeDtypeStruct((b, n, D_KV), BF16),
                   jax.ShapeDtypeStruct((b, n, D_GM), F32),
                   jax.ShapeDtypeStruct((b, n, D_GM), BF16)],
        compiler_params=_params("arbitrary", "arbitrary"),
        name="inproj",
    )(x, sh1, sc1, g_mix, w_in, bd, qg, kg, vg, cos_t, sin_t)


def _ctxkv_kernel(x_ref, sh_ref, sc_ref, g_ref, w_ref, bd_ref, kg_ref, k_ref, v_ref):
    h = _modulate(x_ref[0], g_ref[...], sh_ref[...], sc_ref[...])
    z = _dot(h.astype(BF16), w_ref[:, K_OFF:K_OFF + 2 * D_KV])
    k_ref[0] = _head_norm(z[:, :D_KV], bd_ref[:D_KV, :D_KV], kg_ref[...]).astype(BF16)
    v_ref[0] = z[:, D_KV:].astype(BF16)


def _ctxkv(ctx, csh1, csc1, g_mix, w_in, bd, kg):
    b, n, _ = ctx.shape
    const = lambda i: (0, 0)
    row = lambda i: (i, 0, 0)
    return pl.pallas_call(
        _ctxkv_kernel,
        grid=(b,),
        in_specs=[pl.BlockSpec((1, n, D_MODEL), row),
                  pl.BlockSpec((1, D_MODEL), const),
                  pl.BlockSpec((1, D_MODEL), const),
                  pl.BlockSpec((1, D_MODEL), const),
                  pl.BlockSpec((D_MODEL, D_IN), const),
                  pl.BlockSpec(bd.shape, const),
                  pl.BlockSpec((1, D_KV), const)],
        out_specs=[pl.BlockSpec((1, n, D_KV), row),
                   pl.BlockSpec((1, n, D_KV), row)],
        out_shape=[jax.ShapeDtypeStruct((b, n, D_KV), BF16),
                   jax.ShapeDtypeStruct((b, n, D_KV), BF16)],
        compiler_params=_params("arbitrary"),
        name="ctxkv",
    )(ctx, csh1, csc1, g_mix, w_in, bd, kg)


ATTN_LOOKAHEAD = 4
BOUND_MARGIN = 1.02
MAX_SAFE_SCORE_BOUND = 60.0


def _attn_kernel(q_ref, k_ref, vt_ref, o_ref, st_ref):
    q = q_ref[0]
    k = k_ref[0]
    vt = vt_ref[0]
    def scores(h):
        kv = h // Q_PER_KV
        qh = q[:, h * HEAD_DIM:(h + 1) * HEAD_DIM]
        kh = k[:, kv * HEAD_DIM:(kv + 1) * HEAD_DIM]
        st_ref[h] = _dot_nt(kh, qh)

    for h in range(ATTN_LOOKAHEAD):
        scores(h)
    outs = []
    for h in range(N_Q_HEADS):
        if h + ATTN_LOOKAHEAD < N_Q_HEADS:
            scores(h + ATTN_LOOKAHEAD)
        kv = h // Q_PER_KV
        st = st_ref[h]
        pt = jnp.exp2(st - jnp.max(st, axis=0, keepdims=True))
        l = jnp.sum(pt, axis=0, keepdims=True)
        outs.append(_dot(vt[kv * HEAD_DIM:(kv + 1) * HEAD_DIM, :], pt.astype(BF16)) / l)
    o_ref[0] = jnp.concatenate(outs, axis=0).T.astype(BF16)


def _attn_bounded_kernel(shift_ref, q_ref, k_ref, vt_ref, o_ref, pt_ref):
    q = q_ref[0]
    k = k_ref[0]
    vt = vt_ref[0]
    shift = shift_ref[...]
    sums = [None] * N_Q_HEADS

    def probs(h):
        kv = h // Q_PER_KV
        qh = q[:, h * HEAD_DIM:(h + 1) * HEAD_DIM]
        kh = k[:, kv * HEAD_DIM:(kv + 1) * HEAD_DIM]
        pt = jnp.exp2(_dot_nt(kh, qh) - shift)
        sums[h] = jnp.sum(pt, axis=0, keepdims=True)
        pt_ref[h] = pt.astype(BF16)

    for h in range(ATTN_LOOKAHEAD):
        probs(h)
    outs = []
    for h in range(N_Q_HEADS):
        if h + ATTN_LOOKAHEAD < N_Q_HEADS:
            probs(h + ATTN_LOOKAHEAD)
        kv = h // Q_PER_KV
        outs.append(_dot(vt[kv * HEAD_DIM:(kv + 1) * HEAD_DIM, :], pt_ref[h]) / sums[h])
    o_ref[0] = jnp.concatenate(outs, axis=0).T.astype(BF16)


def _attn(q, k_all, vt_all, score_bound, tile):
    b, n, _ = q.shape
    nk = k_all.shape[1]
    specs = dict(
        grid=(b, n // tile),
        out_specs=pl.BlockSpec((1, tile, D_ATTN), lambda i, j: (i, j, 0)),
        out_shape=jax.ShapeDtypeStruct((b, n, D_ATTN), BF16),
        compiler_params=_params("arbitrary", "arbitrary"))
    qkv_specs = [pl.BlockSpec((1, tile, D_ATTN), lambda i, j: (i, j, 0)),
                 pl.BlockSpec((1, nk, D_KV), lambda i, j: (i, 0, 0)),
                 pl.BlockSpec((1, D_KV, nk), lambda i, j: (i, 0, 0))]

    def exact_max(q, k_all, vt_all, score_bound):
        return pl.pallas_call(
            _attn_kernel, in_specs=qkv_specs,
            scratch_shapes=[pltpu.VMEM((N_Q_HEADS, nk, tile), F32)], name="attn", **specs,
        )(q, k_all, vt_all)

    def bounded(q, k_all, vt_all, score_bound):
        return pl.pallas_call(
            _attn_bounded_kernel, in_specs=[pl.BlockSpec((1, tile), lambda i, j: (0, 0))] + qkv_specs,
            scratch_shapes=[pltpu.VMEM((N_Q_HEADS, nk, tile), BF16)], name="attn_bounded", **specs,
        )(jnp.full((1, tile), score_bound, F32), q, k_all, vt_all)

    return lax.cond(score_bound <= MAX_SAFE_SCORE_BOUND, bounded, exact_max,
                    q, k_all, vt_all, score_bound)


def _mixout_kernel(o_ref, u_ref, vv_ref, x_ref, ws_ref, bs_ref, wout_ref, gt_ref, g_ref,
                   sh_ref, sc_ref, wr_ref, xn_ref, h_ref, aff_ref):
    tile = o_ref.shape[1]
    seg = lax.broadcasted_iota(jnp.int32, (CHUNK, D_GM), 1) // GM_GROUP_DIM
    bs = bs_ref[...]
    gms = []
    for c in range(tile // CHUNK):
        vc = vv_ref[0, c * CHUNK:(c + 1) * CHUNK, :]
        mixed = jnp.zeros((CHUNK, D_GM), F32)
        for g in range(N_GM_GROUPS):
            mixed = jnp.where(seg == g, _dot(ws_ref[g], vc), mixed)
        gms.append(u_ref[0, c * CHUNK:(c + 1) * CHUNK, :] * (mixed + bs))
    gm = jnp.concatenate(gms, axis=0).astype(BF16)

    proj = _dot(o_ref[0], wout_ref[:D_ATTN, :]) + _dot(gm, wout_ref[D_ATTN:, :])
    xn = x_ref[0] + gt_ref[0] * proj
    xn_ref[0] = xn

    h = _modulate(xn, g_ref[...], sh_ref[0], sc_ref[0]).astype(BF16)
    h_ref[0] = h
    logits = _dot_nt(wr_ref[...], h)
    e = jnp.exp(logits - jnp.max(logits, axis=0, keepdims=True))
    aff_ref[0] = e / jnp.sum(e, axis=0, keepdims=True)


def _mixout(o, u, vvn, x, w_s, bs, w_out, gt1, g_ffn, sh2, sc2, w_rt, tile):
    b, n, _ = x.shape
    row = lambda i, j: (i, j, 0)
    per_b = lambda i, j: (i, 0, 0)
    const2 = lambda i, j: (0, 0)
    return pl.pallas_call(
        _mixout_kernel,
        grid=(b, n // tile),
        in_specs=[pl.BlockSpec((1, tile, D_ATTN), row),
                  pl.BlockSpec((1, tile, D_GM), row),
                  pl.BlockSpec((1, tile, D_GM), row),
                  pl.BlockSpec((1, tile, D_MODEL), row),
                  pl.BlockSpec((N_GM_GROUPS, CHUNK, CHUNK), lambda i, j: (0, 0, 0)),
                  pl.BlockSpec((CHUNK, D_GM), const2),
                  pl.BlockSpec((D_ATTN + D_GM, D_MODEL), const2),
                  pl.BlockSpec((1, 1, D_MODEL), per_b),
                  pl.BlockSpec((1, D_MODEL), const2),
                  pl.BlockSpec((1, 1, D_MODEL), per_b),
                  pl.BlockSpec((1, 1, D_MODEL), per_b),
                  pl.BlockSpec((N_EXPERTS, D_MODEL), const2)],
        out_specs=[pl.BlockSpec((1, tile, D_MODEL), row),
                   pl.BlockSpec((1, tile, D_MODEL), row),
                   pl.BlockSpec((1, N_EXPERTS, tile), lambda i, j: (i, 0, j))],
        out_shape=[jax.ShapeDtypeStruct((b, n, D_MODEL), F32),
                   jax.ShapeDtypeStruct((b, n, D_MODEL), BF16),
                   jax.ShapeDtypeStruct((b, N_EXPERTS, n), F32)],
        compiler_params=_params("arbitrary", "arbitrary"),
        name="mixout",
    )(o, u, vvn, x, w_s, bs, w_out, gt1, g_ffn, sh2, sc2, w_rt)


def _cumsum_lanes(m, tri):
    off = jnp.zeros((m.shape[0], 1), F32)
    outs = []
    for blk in range(m.shape[1] // 128):
        mb = m[:, blk * 128:(blk + 1) * 128]
        outs.append(_dot(mb.astype(BF16), tri) + off)
        off = off + jnp.sum(mb, axis=-1, keepdims=True)
    return jnp.concatenate(outs, axis=-1)


REFINE_STEPS = 32


def _route_kernel(cap, aff_ref, tri_ref, pos_ref):
    a = aff_ref[...]
    capf = float(cap)

    def count_ge(t):
        return jnp.sum(jnp.where(a >= t, 1.0, 0.0), axis=-1, keepdims=True)

    def bit_step(i, prefix):
        cand = prefix | jnp.left_shift(jnp.int32(1), 30 - i)
        return jnp.where(count_ge(pltpu.bitcast(cand, F32)) >= capf, cand, prefix)

    prefix = lax.fori_loop(0, 31, bit_step, jnp.zeros((a.shape[0], 1), jnp.int32))
    lo = pltpu.bitcast(prefix, F32)
    hi = pltpu.bitcast(prefix + 1, F32)

    def refine(_, carry):
        lo, hi = carry
        mid = 0.5 * lo + 0.5 * hi
        ok = count_ge(mid) >= capf
        return jnp.where(ok, mid, lo), jnp.where(ok, hi, mid)

    lo, hi = lax.fori_loop(0, REFINE_STEPS, refine, (lo, hi))
    thr = jnp.min(jnp.where(a >= lo, a, jnp.inf), axis=-1, keepdims=True)

    gt = jnp.where(a > thr, 1.0, 0.0)
    eq = jnp.where(a == thr, 1.0, 0.0)
    need = capf - jnp.sum(gt, axis=-1, keepdims=True)
    tri = tri_ref[...]
    sel = gt + eq * jnp.where(_cumsum_lanes(eq, tri) <= need, 1.0, 0.0)
    slot = _cumsum_lanes(sel, tri) - 1.0
    pos_ref[...] = jnp.where(sel > 0.0, slot, -1.0).astype(jnp.int32)


def _route(aff_t, tri, cap, rows):
    b, e, n = aff_t.shape
    pos = pl.pallas_call(
        functools.partial(_route_kernel, cap),
        grid=(b * e // rows,),
        in_specs=[pl.BlockSpec((rows, n), lambda i: (i, 0)),
                  pl.BlockSpec((128, 128), lambda i: (0, 0))],
        out_specs=pl.BlockSpec((rows, n), lambda i: (i, 0)),
        out_shape=jax.ShapeDtypeStruct((b * e, n), jnp.int32),
        compiler_params=_params("arbitrary"),
        name="route",
    )(aff_t.reshape(b * e, n), tri)
    return pos.reshape(b, e, n)


def _gather_kernel(cap, pos_ref, aff_ref, h_ref, xe_ref, gate_ref, tok_ref):
    pos = pos_ref[0, 0]
    n = pos.shape[-1]
    hit = pos == lax.broadcasted_iota(jnp.int32, (cap, n), 0)
    xe_ref[0] = _dot(jnp.where(hit, 1.0, 0.0).astype(BF16), h_ref[0]).astype(BF16)
    gate_ref[0] = jnp.sum(jnp.where(hit, aff_ref[0, 0], 0.0), axis=-1, keepdims=True)
    tok = lax.broadcasted_iota(jnp.int32, (cap, n), 1).astype(F32)
    tok_ref[0] = jnp.sum(jnp.where(hit, tok, 0.0), axis=-1, keepdims=True).astype(jnp.int32)


def _gather(pos_t, aff_t, h2, cap):
    b, e, n = pos_t.shape
    row = lambda i, j: (i, j, 0, 0)
    slots = lambda i, j: (j, i, 0)
    return pl.pallas_call(
        functools.partial(_gather_kernel, cap),
        grid=(b, e),
        in_specs=[pl.BlockSpec((1, 1, 1, n), row),
                  pl.BlockSpec((1, 1, 1, n), row),
                  pl.BlockSpec((1, n, D_MODEL), lambda i, j: (i, 0, 0))],
        out_specs=[pl.BlockSpec((1, cap, D_MODEL), slots),
                   pl.BlockSpec((1, cap, 1), slots),
                   pl.BlockSpec((1, cap, 1), slots)],
        out_shape=[jax.ShapeDtypeStruct((e, b * cap, D_MODEL), BF16),
                   jax.ShapeDtypeStruct((e, b * cap, 1), F32),
                   jax.ShapeDtypeStruct((e, b * cap, 1), jnp.int32)],
        compiler_params=_params("arbitrary", "arbitrary"),
        name="gather",
    )(pos_t.reshape(b, e, 1, n), aff_t.reshape(b, e, 1, n), h2)


FF_CHUNK = 512


def _ffn_kernel(n_experts, xe_ref, gate_ref, w1_ref, w3_ref, w2_ref, y_ref, w1b, w3b, w2b):
    i = pl.program_id(0)
    j = pl.program_id(1)
    rows13 = w1_ref.shape[1]
    rows2 = w2_ref.shape[1]
    tile = xe_ref.shape[1]

    @pl.when(i < n_experts)
    def _():
        slot = i % 2
        w1b[slot, pl.ds(j * rows13, rows13), :] = w1_ref[0].astype(BF16)
        w3b[slot, pl.ds(j * rows13, rows13), :] = w3_ref[0].astype(BF16)
        w2b[slot, pl.ds(j * rows2, rows2), :] = w2_ref[0].astype(BF16)

    @pl.when(i == 0)
    def _():
        y_ref[...] = jnp.zeros_like(y_ref)

    @pl.when(i > 0)
    def _():
        slot = (i - 1) % 2
        xe = xe_ref[0]
        acc = jnp.zeros((tile, D_MODEL), F32)
        for c in range(EXPERT_FF // FF_CHUNK):
            cols = slice(c * FF_CHUNK, (c + 1) * FF_CHUNK)
            a = _dot(xe, w1b[slot, :, cols])
            g = _dot(xe, w3b[slot, :, cols])
            hid = (a / (1.0 + jnp.exp(-a))) * g
            acc = acc + _dot(hid.astype(BF16), w2b[slot, cols, :])
        y = acc * gate_ref[0]
        for s in range(SLABS):
            y_ref[0, pl.ds(s, tile, stride=SLABS), :] = y[:, s * LANES:(s + 1) * LANES]


def _ffn(xe, gate, w1, w3, w2, tile):
    e, m, _ = xe.shape
    steps = m // tile
    cur = lambda i, j: (jnp.maximum(i - 1, 0), j, 0)
    nxt = lambda i, j: (jnp.minimum(i, e - 1), j, 0)
    return pl.pallas_call(
        functools.partial(_ffn_kernel, e),
        grid=(e + 1, steps),
        in_specs=[pl.BlockSpec((1, tile, D_MODEL), cur),
                  pl.BlockSpec((1, tile, 1), cur),
                  pl.BlockSpec((1, D_MODEL // steps, EXPERT_FF), nxt),
                  pl.BlockSpec((1, D_MODEL // steps, EXPERT_FF), nxt),
                  pl.BlockSpec((1, EXPERT_FF // steps, D_MODEL), nxt)],
        out_specs=pl.BlockSpec((1, tile * SLABS, LANES), lambda i, j: (i, j, 0)),
        out_shape=jax.ShapeDtypeStruct((e + 1, m * SLABS, LANES), F32),
        scratch_shapes=[pltpu.VMEM((2, D_MODEL, EXPERT_FF), BF16),
                        pltpu.VMEM((2, D_MODEL, EXPERT_FF), BF16),
                        pltpu.VMEM((2, EXPERT_FF, D_MODEL), BF16)],
        compiler_params=_params("arbitrary", "arbitrary"),
        name="ffn",
    )(xe, gate, w1, w3, w2)


SCATTER_UNROLL = 8
FINAL_TILE = 256


def _combine_kernel(cap, n_batch, n_experts, tok_ref, y_ref, xn_ref, gt_ref, g_ref, o_ref, acc_ref):
    b = pl.program_id(0)
    k = pl.program_id(1)

    @pl.when(k == 0)
    def _():
        acc_ref[...] = jnp.zeros_like(acc_ref)

    def tile_at(i):
        return pl.ds(pl.multiple_of(i * SLABS, SLABS), SLABS)

    @pl.when(k < n_experts)
    def _():
        base = (k * n_batch + b) * cap

        def group(gi, carry):
            rows, vals = [], []
            for u in range(SCATTER_UNROLL):
                r = gi * SCATTER_UNROLL + u
                row = tile_at(tok_ref[base + r])
                rows.append(row)
                vals.append(acc_ref[row, :] + y_ref[0, tile_at(r), :])
            for row, val in zip(rows, vals):
                acc_ref[row, :] = val
            return carry

        lax.fori_loop(0, cap // SCATTER_UNROLL, group, 0)

    @pl.when(k >= n_experts)
    def _():
        start = (k - n_experts) * (FINAL_TILE * SLABS)
        moe = jnp.concatenate(
            [acc_ref[pl.ds(start + s, FINAL_TILE, stride=SLABS), :] for s in range(SLABS)],
            axis=-1)
        z = xn_ref[0] + gt_ref[0] * moe
        o_ref[0] = (z * _rms(z)) * g_ref[...]


def _combine(y, tok, xn, gt2, g_final, cap):
    b, n, _ = xn.shape
    e = y.shape[0] - 1
    rows = lambda i, k, tok: (i, jnp.maximum(k - e, 0), 0)
    return pl.pallas_call(
        functools.partial(_combine_kernel, cap, b, e),
        grid_spec=pltpu.PrefetchScalarGridSpec(
            num_scalar_prefetch=1,
            grid=(b, e + n // FINAL_TILE),
            in_specs=[pl.BlockSpec((1, cap * SLABS, LANES),
                                   lambda i, k, tok: (jnp.minimum(k, e - 1) + 1, i, 0)),
                      pl.BlockSpec((1, FINAL_TILE, D_MODEL), rows),
                      pl.BlockSpec((1, 1, D_MODEL), lambda i, k, tok: (i, 0, 0)),
                      pl.BlockSpec((1, D_MODEL), lambda i, k, tok: (0, 0))],
            out_specs=pl.BlockSpec((1, FINAL_TILE, D_MODEL), rows),
            scratch_shapes=[pltpu.VMEM((n * SLABS, LANES), F32)]),
        out_shape=jax.ShapeDtypeStruct((b, n, D_MODEL), F32),
        compiler_params=_params("arbitrary", "arbitrary"),
        name="combine",
    )(tok, y, xn, gt2, g_final)


def _rope_tables(n):
    pos = jnp.arange(n)
    n_freq = HEAD_DIM // 4
    inv = ROPE_THETA ** (-jnp.arange(n_freq, dtype=F32) / n_freq)
    ang = jnp.concatenate([(pos // GRID_W)[:, None].astype(F32) * inv,
                           (pos % GRID_W)[:, None].astype(F32) * inv], axis=-1)
    cos, sin = jnp.cos(ang), jnp.sin(ang)
    cos_t = jnp.tile(jnp.concatenate([cos, cos], axis=-1), (1, N_Q_HEADS))
    sin_t = jnp.tile(jnp.concatenate([-sin, sin], axis=-1), (1, N_Q_HEADS))
    return cos_t, sin_t


def _block_diag_ones(width, block):
    i = jnp.arange(width) // block
    return (i[:, None] == i[None, :]).astype(BF16)


def kernel(x, c, ctx, c_ctx, w_mod, b_mod, g_mix, g_ffn, w_in, q_gain, k_gain, v_gain, w_s, b_s,
           w_out, w_router, w1, w3, w2, g_final):
    assert w_mod.shape[0] == 1, "single-layer kernel"
    b, n, d = x.shape
    cap = CAPACITY_FACTOR * n // N_EXPERTS

    rows = -(-(b + 1) // 8) * 8
    cvecs = jnp.zeros((rows, d), F32).at[:b].set(c).at[b].set(c_ctx)
    mods = _adaln(cvecs, w_mod[0], b_mod[0])
    sh1, sc1, gt1, sh2, sc2, gt2 = [m[:b, None, :] for m in jnp.split(mods, 6, axis=-1)]
    csh1, csc1 = mods[b:b + 1, :d], mods[b:b + 1, d:2 * d]

    cos_t, sin_t = _rope_tables(n)
    bd = _block_diag_ones(256, HEAD_DIM)
    w_in_b = w_in[0].astype(BF16)
    g_mix2 = g_mix[0][None, :]
    qg = jnp.tile(q_gain[0], N_Q_HEADS)[None, :]
    kg = jnp.tile(k_gain[0], N_KV_HEADS)[None, :]

    q, kl, vl, u, vvn = _inproj(x, sh1, sc1, g_mix2, w_in_b, bd, qg, kg, v_gain[0][None, :],
                                cos_t, sin_t, tile=256)
    kc, vc = _ctxkv(ctx, csh1, csc1, g_mix2, w_in_b, bd, kg)
    vt_all = jnp.swapaxes(jnp.concatenate([vl, vc], axis=1), 1, 2)
    score_bound = (BOUND_MARGIN * HEAD_DIM * Q_SCALE) * jnp.max(jnp.abs(q_gain[0])) * jnp.max(jnp.abs(k_gain[0]))
    o = _attn(q, jnp.concatenate([kl, kc], axis=1), vt_all, score_bound, tile=256)

    bs = jnp.repeat(b_s[0].T, GM_GROUP_DIM, axis=1)
    xn, h2, aff_t = _mixout(o, u, vvn, x, w_s[0].astype(BF16), bs, w_out[0].astype(BF16), gt1,
                            g_ffn[0][None, :], sh2, sc2, w_router[0].T.astype(BF16), tile=256)

    tri = (jnp.arange(128)[:, None] <= jnp.arange(128)[None, :]).astype(BF16)
    pos_t = _route(aff_t, tri, cap, rows=64)
    xe, gate, tok = _gather(pos_t, aff_t, h2, cap)
    y = _ffn(xe, gate, w1[0], w3[0], w2[0], tile=512)
    return _combine(y, tok.reshape(-1), xn, gt2, g_final[None, :], cap)
```

```python
import functools
import math

import jax
import jax.numpy as jnp
from jax import lax
from jax.experimental import pallas as pl
from jax.experimental.pallas import tpu as pltpu

D_MODEL = 1024
CTX_LEN = 256
GRID_W = 64
HEAD_DIM = 64
N_Q_HEADS = 8
N_KV_HEADS = 2
Q_PER_KV = N_Q_HEADS // N_KV_HEADS
D_ATTN = N_Q_HEADS * HEAD_DIM
D_KV = N_KV_HEADS * HEAD_DIM
N_GM_GROUPS = 8
GM_GROUP_DIM = 64
D_GM = N_GM_GROUPS * GM_GROUP_DIM
CHUNK = 128
D_IN = D_ATTN + 2 * D_KV + 2 * D_GM
ROPE_THETA = 10000.0
N_EXPERTS = 16
CAPACITY_FACTOR = 2
EXPERT_FF = 2048
EPS = 1e-6

F32 = jnp.float32
BF16 = jnp.bfloat16

VMEM_LIMIT_BYTES = 52 * 1024 * 1024
LANES = 128
SLABS = D_MODEL // LANES

K_OFF = D_ATTN
V_OFF = D_ATTN + D_KV
GM_OFF = D_ATTN + 2 * D_KV

Q_SCALE = math.log2(math.e) * HEAD_DIM ** -0.5


def _params(*sem):
    return pltpu.CompilerParams(dimension_semantics=sem, vmem_limit_bytes=VMEM_LIMIT_BYTES)


def _dot(a, b):
    return jnp.dot(a, b, preferred_element_type=F32)


def _dot_nt(a, b):
    return lax.dot_general(a, b, (((1,), (1,)), ((), ())), preferred_element_type=F32)


def _rms(v):
    return lax.rsqrt(jnp.sum(v * v, axis=-1, keepdims=True) * (1.0 / v.shape[-1]) + EPS)


def _modulate(x, g, shift, scale):
    return ((x * _rms(x)) * g) * (1.0 + scale) + shift


def _seg_sum(v, bd):
    w = bd.shape[0]
    outs = []
    for j in range(v.shape[1] // w):
        c = v[:, j * w:(j + 1) * w]
        hi = c.astype(BF16)
        lo = (c - hi.astype(F32)).astype(BF16)
        outs.append(_dot(hi, bd) + _dot(lo, bd))
    return outs[0] if len(outs) == 1 else jnp.concatenate(outs, axis=-1)


def _head_norm(z, bd, gain):
    ss = _seg_sum(z * z, bd)
    return (z * lax.rsqrt(ss * (1.0 / HEAD_DIM) + EPS)) * gain


def _rope(v, cos, sin_signed):
    w = v.shape[-1]
    lane = lax.broadcasted_iota(jnp.int32, v.shape, 1)
    partner = jnp.where((lane & (HEAD_DIM // 2)) == 0,
                        pltpu.roll(v, w - HEAD_DIM // 2, axis=1),
                        pltpu.roll(v, HEAD_DIM // 2, axis=1))
    return v * cos + partner * sin_signed


def _gelu(z):
    return 0.5 * z * (1.0 + lax.erf(z * math.sqrt(0.5)))


def _adaln_kernel(c_ref, w_ref, b_ref, o_ref):
    c = c_ref[...]
    s = c / (1.0 + jnp.exp(-c))
    o_ref[...] = _dot(s.astype(BF16), w_ref[...].astype(BF16)) + b_ref[...]


def _adaln(cvecs, w_mod, b_mod):
    rows = cvecs.shape[0]
    n_out = w_mod.shape[1]
    tn = 768
    return pl.pallas_call(
        _adaln_kernel,
        grid=(n_out // tn,),
        in_specs=[pl.BlockSpec((rows, D_MODEL), lambda j: (0, 0)),
                  pl.BlockSpec((D_MODEL, tn), lambda j: (0, j)),
                  pl.BlockSpec((1, tn), lambda j: (0, j))],
        out_specs=pl.BlockSpec((rows, tn), lambda j: (0, j)),
        out_shape=jax.ShapeDtypeStruct((rows, n_out), F32),
        compiler_params=_params("arbitrary"),
        name="adaln",
    )(cvecs, w_mod, b_mod.reshape(1, n_out))


def _inproj_kernel(x_ref, sh_ref, sc_ref, g_ref, w_ref, bd_ref, qg_ref, kg_ref, vg_ref,
                   cos_ref, sin_ref, q_ref, k_ref, v_ref, u_ref, vv_ref):
    h = _modulate(x_ref[0], g_ref[...], sh_ref[0], sc_ref[0])
    z = _dot(h.astype(BF16), w_ref[...])
    cos = cos_ref[...]
    sin = sin_ref[...]
    bd = bd_ref[...]

    q = _rope(_head_norm(z[:, :D_ATTN], bd, qg_ref[...]), cos, sin)
    q_ref[0] = (q * Q_SCALE).astype(BF16)

    k = _head_norm(z[:, K_OFF:K_OFF + D_KV], bd[:D_KV, :D_KV], kg_ref[...])
    k_ref[0] = _rope(k, cos[:, :D_KV], sin[:, :D_KV]).astype(BF16)
    v_ref[0] = z[:, V_OFF:V_OFF + D_KV].astype(BF16)

    gz = _gelu(z[:, GM_OFF:])
    u_ref[0] = gz[:, :D_GM]
    vv = gz[:, D_GM:]
    vv_ref[0] = ((vv * _rms(vv)) * vg_ref[...]).astype(BF16)


def _inproj(x, sh1, sc1, g_mix, w_in, bd, qg, kg, vg, cos_t, sin_t, tile):
    b, n, _ = x.shape
    row = lambda i, j: (j, i, 0)
    per_b = lambda i, j: (j, 0, 0)
    const = lambda i, j: (0, 0)
    tab = lambda i, j: (i, 0)
    return pl.pallas_call(
        _inproj_kernel,
        grid=(n // tile, b),
        in_specs=[pl.BlockSpec((1, tile, D_MODEL), row),
                  pl.BlockSpec((1, 1, D_MODEL), per_b),
                  pl.BlockSpec((1, 1, D_MODEL), per_b),
                  pl.BlockSpec((1, D_MODEL), const),
                  pl.BlockSpec((D_MODEL, D_IN), const),
                  pl.BlockSpec(bd.shape, const),
                  pl.BlockSpec((1, D_ATTN), const),
                  pl.BlockSpec((1, D_KV), const),
                  pl.BlockSpec((1, D_GM), const),
                  pl.BlockSpec((tile, D_ATTN), tab),
                  pl.BlockSpec((tile, D_ATTN), tab)],
        out_specs=[pl.BlockSpec((1, tile, D_ATTN), row),
                   pl.BlockSpec((1, tile, D_KV), row),
                   pl.BlockSpec((1, tile, D_KV), row),
                   pl.BlockSpec((1, tile, D_GM), row),
                   pl.BlockSpec((1, tile, D_GM), row)],
        out_shape=[jax.ShapeDtypeStruct((b, n, D_ATTN), BF16),
                   jax.ShapeDtypeStruct((b, n, D_KV), BF16),
                   jax.ShapeDtypeStruct((b, n, D_KV), BF16),
                   jax.ShapeDtypeStruct((b, n, D_GM), F32),
                   jax.ShapeDtypeStruct((b, n, D_GM), BF16)],
        compiler_params=_params("arbitrary", "arbitrary"),
        name="inproj",
    )(x, sh1, sc1, g_mix, w_in, bd, qg, kg, vg, cos_t, sin_t)


def _ctxkv_kernel(x_ref, sh_ref, sc_ref, g_ref, w_ref, bd_ref, kg_ref, k_ref, v_ref):
    h = _modulate(x_ref[0], g_ref[...], sh_ref[...], sc_ref[...])
    z = _dot(h.astype(BF16), w_ref[:, K_OFF:K_OFF + 2 * D_KV])
    k_ref[0] = _head_norm(z[:, :D_KV], bd_ref[:D_KV, :D_KV], kg_ref[...]).astype(BF16)
    v_ref[0] = z[:, D_KV:].astype(BF16)


def _ctxkv(ctx, csh1, csc1, g_mix, w_in, bd, kg):
    b, n, _ = ctx.shape
    const = lambda i: (0, 0)
    row = lambda i: (i, 0, 0)
    return pl.pallas_call(
        _ctxkv_kernel,
        grid=(b,),
        in_specs=[pl.BlockSpec((1, n, D_MODEL), row),
                  pl.BlockSpec((1, D_MODEL), const),
                  pl.BlockSpec((1, D_MODEL), const),
                  pl.BlockSpec((1, D_MODEL), const),
                  pl.BlockSpec((D_MODEL, D_IN), const),
                  pl.BlockSpec(bd.shape, const),
                  pl.BlockSpec((1, D_KV), const)],
        out_specs=[pl.BlockSpec((1, n, D_KV), row),
                   pl.BlockSpec((1, n, D_KV), row)],
        out_shape=[jax.ShapeDtypeStruct((b, n, D_KV), BF16),
                   jax.ShapeDtypeStruct((b, n, D_KV), BF16)],
        compiler_params=_params("arbitrary"),
        name="ctxkv",
    )(ctx, csh1, csc1, g_mix, w_in, bd, kg)


ATTN_LOOKAHEAD = 4
BOUND_MARGIN = 1.02
MAX_SAFE_SCORE_BOUND = 60.0


def _attn_kernel(q_ref, k_ref, vt_ref, o_ref, st_ref):
    q = q_ref[0]
    k = k_ref[0]
    vt = vt_ref[0]
    def scores(h):
        kv = h // Q_PER_KV
        qh = q[:, h * HEAD_DIM:(h + 1) * HEAD_DIM]
        kh = k[:, kv * HEAD_DIM:(kv + 1) * HEAD_DIM]
        st_ref[h] = _dot_nt(kh, qh)

    for h in range(ATTN_LOOKAHEAD):
        scores(h)
    outs = []
    for h in range(N_Q_HEADS):
        if h + ATTN_LOOKAHEAD < N_Q_HEADS:
            scores(h + ATTN_LOOKAHEAD)
        kv = h // Q_PER_KV
        st = st_ref[h]
        pt = jnp.exp2(st - jnp.max(st, axis=0, keepdims=True))
        l = jnp.sum(pt, axis=0, keepdims=True)
        outs.append(_dot(vt[kv * HEAD_DIM:(kv + 1) * HEAD_DIM, :], pt.astype(BF16)) / l)
    o_ref[0] = jnp.concatenate(outs, axis=0).T.astype(BF16)


def _attn_bounded_kernel(shift_ref, q_ref, k_ref, vt_ref, o_ref, pt_ref):
    q = q_ref[0]
    k = k_ref[0]
    vt = vt_ref[0]
    shift = shift_ref[...]
    sums = [None] * N_Q_HEADS

    def probs(h):
        kv = h // Q_PER_KV
        qh = q[:, h * HEAD_DIM:(h + 1) * HEAD_DIM]
        kh = k[:, kv * HEAD_DIM:(kv + 1) * HEAD_DIM]
        pt = jnp.exp2(_dot_nt(kh, qh) - shift)
        sums[h] = jnp.sum(pt, axis=0, keepdims=True)
        pt_ref[h] = pt.astype(BF16)

    for h in range(ATTN_LOOKAHEAD):
        probs(h)
    outs = []
    for h in range(N_Q_HEADS):
        if h + ATTN_LOOKAHEAD < N_Q_HEADS:
            probs(h + ATTN_LOOKAHEAD)
        kv = h // Q_PER_KV
        outs.append(_dot(vt[kv * HEAD_DIM:(kv + 1) * HEAD_DIM, :], pt_ref[h]) / sums[h])
    o_ref[0] = jnp.concatenate(outs, axis=0).T.astype(BF16)


def _attn(q, k_all, vt_all, score_bound, tile):
    b, n, _ = q.shape
    nk = k_all.shape[1]
    specs = dict(
        grid=(b, n // tile),
        out_specs=pl.BlockSpec((1, tile, D_ATTN), lambda i, j: (i, j, 0)),
        out_shape=jax.ShapeDtypeStruct((b, n, D_ATTN), BF16),
        compiler_params=_params("arbitrary", "arbitrary"))
    qkv_specs = [pl.BlockSpec((1, tile, D_ATTN), lambda i, j: (i, j, 0)),
                 pl.BlockSpec((1, nk, D_KV), lambda i, j: (i, 0, 0)),
                 pl.BlockSpec((1, D_KV, nk), lambda i, j: (i, 0, 0))]

    def exact_max(q, k_all, vt_all, score_bound):
        return pl.pallas_call(
            _attn_kernel, in_specs=qkv_specs,
            scratch_shapes=[pltpu.VMEM((N_Q_HEADS, nk, tile), F32)], name="attn", **specs,
        )(q, k_all, vt_all)

    def bounded(q, k_all, vt_all, score_bound):
        return pl.pallas_call(
            _attn_bounded_kernel, in_specs=[pl.BlockSpec((1, tile), lambda i, j: (0, 0))] + qkv_specs,
            scratch_shapes=[pltpu.VMEM((N_Q_HEADS, nk, tile), BF16)], name="attn_bounded", **specs,
        )(jnp.full((1, tile), score_bound, F32), q, k_all, vt_all)

    return lax.cond(score_bound <= MAX_SAFE_SCORE_BOUND, bounded, exact_max,
                    q, k_all, vt_all, score_bound)


def _mixout_kernel(o_ref, u_ref, vv_ref, x_ref, ws_ref, bs_ref, wout_ref, gt_ref, g_ref,
                   sh_ref, sc_ref, wr_ref, xn_ref, h_ref, aff_ref):
    tile = o_ref.shape[1]
    seg = lax.broadcasted_iota(jnp.int32, (CHUNK, D_GM), 1) // GM_GROUP_DIM
    bs = bs_ref[...]
    gms = []
    for c in range(tile // CHUNK):
        vc = vv_ref[0, c * CHUNK:(c + 1) * CHUNK, :]
        mixed = jnp.zeros((CHUNK, D_GM), F32)
        for g in range(N_GM_GROUPS):
            mixed = jnp.where(seg == g, _dot(ws_ref[g], vc), mixed)
        gms.append(u_ref[0, c * CHUNK:(c + 1) * CHUNK, :] * (mixed + bs))
    gm = jnp.concatenate(gms, axis=0).astype(BF16)

    proj = _dot(o_ref[0], wout_ref[:D_ATTN, :]) + _dot(gm, wout_ref[D_ATTN:, :])
    xn = x_ref[0] + gt_ref[0] * proj
    xn_ref[0] = xn

    h = _modulate(xn, g_ref[...], sh_ref[0], sc_ref[0]).astype(BF16)
    h_ref[0] = h
    logits = _dot_nt(wr_ref[...], h)
    e = jnp.exp(logits - jnp.max(logits, axis=0, keepdims=True))
    aff_ref[0] = e / jnp.sum(e, axis=0, keepdims=True)


def _mixout(o, u, vvn, x, w_s, bs, w_out, gt1, g_ffn, sh2, sc2, w_rt, tile):
    b, n, _ = x.shape
    row = lambda i, j: (i, j, 0)
    per_b = lambda i, j: (i, 0, 0)
    const2 = lambda i, j: (0, 0)
    return pl.pallas_call(
        _mixout_kernel,
        grid=(b, n // tile),
        in_specs=[pl.BlockSpec((1, tile, D_ATTN), row),
                  pl.BlockSpec((1, tile, D_GM), row),
                  pl.BlockSpec((1, tile, D_GM), row),
                  pl.BlockSpec((1, tile, D_MODEL), row),
                  pl.BlockSpec((N_GM_GROUPS, CHUNK, CHUNK), lambda i, j: (0, 0, 0)),
                  pl.BlockSpec((CHUNK, D_GM), const2),
                  pl.BlockSpec((D_ATTN + D_GM, D_MODEL), const2),
                  pl.BlockSpec((1, 1, D_MODEL), per_b),
                  pl.BlockSpec((1, D_MODEL), const2),
                  pl.BlockSpec((1, 1, D_MODEL), per_b),
                  pl.BlockSpec((1, 1, D_MODEL), per_b),
                  pl.BlockSpec((N_EXPERTS, D_MODEL), const2)],
        out_specs=[pl.BlockSpec((1, tile, D_MODEL), row),
                   pl.BlockSpec((1, tile, D_MODEL), row),
                   pl.BlockSpec((1, N_EXPERTS, tile), lambda i, j: (i, 0, j))],
        out_shape=[jax.ShapeDtypeStruct((b, n, D_MODEL), F32),
                   jax.ShapeDtypeStruct((b, n, D_MODEL), BF16),
                   jax.ShapeDtypeStruct((b, N_EXPERTS, n), F32)],
        compiler_params=_params("arbitrary", "arbitrary"),
        name="mixout",
    )(o, u, vvn, x, w_s, bs, w_out, gt1, g_ffn, sh2, sc2, w_rt)


def _cumsum_lanes(m, tri):
    off = jnp.zeros((m.shape[0], 1), F32)
    outs = []
    for blk in range(m.shape[1] // 128):
        mb = m[:, blk * 128:(blk + 1) * 128]
        outs.append(_dot(mb.astype(BF16), tri) + off)
        off = off + jnp.sum(mb, axis=-1, keepdims=True)
    return jnp.concatenate(outs, axis=-1)


REFINE_STEPS = 32


def _route_kernel(cap, aff_ref, tri_ref, pos_ref):
    a = aff_ref[...]
    capf = float(cap)

    def count_ge(t):
        return jnp.sum(jnp.where(a >= t, 1.0, 0.0), axis=-1, keepdims=True)

    def bit_step(i, prefix):
        cand = prefix | jnp.left_shift(jnp.int32(1), 30 - i)
        return jnp.where(count_ge(pltpu.bitcast(cand, F32)) >= capf, cand, prefix)

    prefix = lax.fori_loop(0, 31, bit_step, jnp.zeros((a.shape[0], 1), jnp.int32))
    lo = pltpu.bitcast(prefix, F32)
    hi = pltpu.bitcast(prefix + 1, F32)

    def refine(_, carry):
        lo, hi = carry
        mid = 0.5 * lo + 0.5 * hi
        ok = count_ge(mid) >= capf
        return jnp.where(ok, mid, lo), jnp.where(ok, hi, mid)

    lo, hi = lax.fori_loop(0, REFINE_STEPS, refine, (lo, hi))
    thr = jnp.min(jnp.where(a >= lo, a, jnp.inf), axis=-1, keepdims=True)

    gt = jnp.where(a > thr, 1.0, 0.0)
    eq = jnp.where(a == thr, 1.0, 0.0)
    need = capf - jnp.sum(gt, axis=-1, keepdims=True)
    tri = tri_ref[...]
    sel = gt + eq * jnp.where(_cumsum_lanes(eq, tri) <= need, 1.0, 0.0)
    slot = _cumsum_lanes(sel, tri) - 1.0
    pos_ref[...] = jnp.where(sel > 0.0, slot, -1.0).astype(jnp.int32)


def _route(aff_t, tri, cap, rows):
    b, e, n = aff_t.shape
    pos = pl.pallas_call(
        functools.partial(_route_kernel, cap),
        grid=(b * e // rows,),
        in_specs=[pl.BlockSpec((rows, n), lambda i: (i, 0)),
                  pl.BlockSpec((128, 128), lambda i: (0, 0))],
        out_specs=pl.BlockSpec((rows, n), lambda i: (i, 0)),
        out_shape=jax.ShapeDtypeStruct((b * e, n), jnp.int32),
        compiler_params=_params("arbitrary"),
        name="route",
    )(aff_t.reshape(b * e, n), tri)
    return pos.reshape(b, e, n)


GATHER_EXPERTS = 4


def _gather_kernel(cap, pos_ref, aff_ref, h_ref, xe_ref, gate_ref, tok_ref):
    n = pos_ref.shape[-1]
    slot = lax.broadcasted_iota(jnp.int32, (cap, n), 0)
    tok = lax.broadcasted_iota(jnp.int32, (cap, n), 1).astype(F32)
    for x in range(GATHER_EXPERTS):
        hit = pos_ref[0, x] == slot
        xe_ref[x] = _dot(jnp.where(hit, 1.0, 0.0).astype(BF16), h_ref[0]).astype(BF16)
        gate_ref[x] = jnp.sum(jnp.where(hit, aff_ref[0, x], 0.0), axis=-1, keepdims=True)
        tok_ref[x] = jnp.sum(jnp.where(hit, tok, 0.0), axis=-1, keepdims=True).astype(jnp.int32)


def _gather(pos_t, aff_t, h2, cap):
    b, e, n = pos_t.shape
    row = lambda i, j: (i, j, 0, 0)
    slots = lambda i, j: (j, i, 0)
    return pl.pallas_call(
        functools.partial(_gather_kernel, cap),
        grid=(b, e // GATHER_EXPERTS),
        in_specs=[pl.BlockSpec((1, GATHER_EXPERTS, 1, n), row),
                  pl.BlockSpec((1, GATHER_EXPERTS, 1, n), row),
                  pl.BlockSpec((1, n, D_MODEL), lambda i, j: (i, 0, 0))],
        out_specs=[pl.BlockSpec((GATHER_EXPERTS, cap, D_MODEL), slots),
                   pl.BlockSpec((GATHER_EXPERTS, cap, 1), slots),
                   pl.BlockSpec((GATHER_EXPERTS, cap, 1), slots)],
        out_shape=[jax.ShapeDtypeStruct((e, b * cap, D_MODEL), BF16),
                   jax.ShapeDtypeStruct((e, b * cap, 1), F32),
                   jax.ShapeDtypeStruct((e, b * cap, 1), jnp.int32)],
        compiler_params=_params("arbitrary", "arbitrary"),
        name="gather",
    )(pos_t.reshape(b, e, 1, n), aff_t.reshape(b, e, 1, n), h2)


FF_CHUNK = 512


def _ffn_kernel(n_experts, xe_ref, gate_ref, w1_ref, w3_ref, w2_ref, y_ref, w1b, w3b, w2b):
    i = pl.program_id(0)
    j = pl.program_id(1)
    rows13 = w1_ref.shape[1]
    rows2 = w2_ref.shape[1]
    tile = xe_ref.shape[1]

    @pl.when(i < n_experts)
    def _():
        slot = i % 2
        w1b[slot, pl.ds(j * rows13, rows13), :] = w1_ref[0].astype(BF16)
        w3b[slot, pl.ds(j * rows13, rows13), :] = w3_ref[0].astype(BF16)
        w2b[slot, pl.ds(j * rows2, rows2), :] = w2_ref[0].astype(BF16)

    @pl.when(i == 0)
    def _():
        y_ref[...] = jnp.zeros_like(y_ref)

    @pl.when(i > 0)
    def _():
        slot = (i - 1) % 2
        xe = xe_ref[0]
        acc = jnp.zeros((tile, D_MODEL), F32)
        for c in range(EXPERT_FF // FF_CHUNK):
            cols = slice(c * FF_CHUNK, (c + 1) * FF_CHUNK)
            a = _dot(xe, w1b[slot, :, cols])
            g = _dot(xe, w3b[slot, :, cols])
            hid = (a / (1.0 + jnp.exp(-a))) * g
            acc = acc + _dot(hid.astype(BF16), w2b[slot, cols, :])
        y = acc * gate_ref[0]
        for s in range(SLABS):
            y_ref[0, pl.ds(s, tile, stride=SLABS), :] = y[:, s * LANES:(s + 1) * LANES]


def _ffn(xe, gate, w1, w3, w2, tile):
    e, m, _ = xe.shape
    steps = m // tile
    cur = lambda i, j: (jnp.maximum(i - 1, 0), j, 0)
    nxt = lambda i, j: (jnp.minimum(i, e - 1), j, 0)
    return pl.pallas_call(
        functools.partial(_ffn_kernel, e),
        grid=(e + 1, steps),
        in_specs=[pl.BlockSpec((1, tile, D_MODEL), cur),
                  pl.BlockSpec((1, tile, 1), cur),
                  pl.BlockSpec((1, D_MODEL // steps, EXPERT_FF), nxt),
                  pl.BlockSpec((1, D_MODEL // steps, EXPERT_FF), nxt),
                  pl.BlockSpec((1, EXPERT_FF // steps, D_MODEL), nxt)],
        out_specs=pl.BlockSpec((1, tile * SLABS, LANES), lambda i, j: (jnp.where(i == 0, e, i - 1), j, 0)),
        out_shape=jax.ShapeDtypeStruct((e + 1, m * SLABS, LANES), F32),
        scratch_shapes=[pltpu.VMEM((2, D_MODEL, EXPERT_FF), BF16),
                        pltpu.VMEM((2, D_MODEL, EXPERT_FF), BF16),
                        pltpu.VMEM((2, EXPERT_FF, D_MODEL), BF16)],
        compiler_params=_params("arbitrary", "arbitrary"),
        name="ffn",
    )(xe, gate, w1, w3, w2)


SCATTER_UNROLL = 8
SCATTER_EXPERTS = 4
FINAL_TILE = 512


def _combine_kernel(cap, n_batch, n_scatter, tok_ref, y_ref, xn_ref, gt_ref, g_ref, o_ref, acc_ref):
    b = pl.program_id(0)
    k = pl.program_id(1)

    @pl.when(k == 0)
    def _():
        acc_ref[...] = jnp.zeros_like(acc_ref)

    def tile_at(i):
        return pl.ds(pl.multiple_of(i * SLABS, SLABS), SLABS)

    def scatter_expert(x):
        base = ((k * SCATTER_EXPERTS + x) * n_batch + b) * cap

        def group(gi, carry):
            rows, vals = [], []
            for u in range(SCATTER_UNROLL):
                r = gi * SCATTER_UNROLL + u
                row = tile_at(tok_ref[base + r])
                rows.append(row)
                vals.append(acc_ref[row, :] + y_ref[x, tile_at(r), :])
            for row, val in zip(rows, vals):
                acc_ref[row, :] = val
            return carry

        lax.fori_loop(0, cap // SCATTER_UNROLL, group, 0)

    @pl.when(k < n_scatter)
    def _():
        for x in range(SCATTER_EXPERTS):
            scatter_expert(x)

    @pl.when(k >= n_scatter)
    def _():
        start = (k - n_scatter) * (FINAL_TILE * SLABS)
        moe = jnp.concatenate(
            [acc_ref[pl.ds(start + s, FINAL_TILE, stride=SLABS), :] for s in range(SLABS)],
            axis=-1)
        z = xn_ref[0] + gt_ref[0] * moe
        o_ref[0] = (z * _rms(z)) * g_ref[...]


def _combine(y, tok, xn, gt2, g_final, cap):
    b, n, _ = xn.shape
    e = y.shape[0] - 1
    n_scatter = e // SCATTER_EXPERTS
    rows = lambda i, k, tok: (i, jnp.maximum(k - n_scatter, 0), 0)
    return pl.pallas_call(
        functools.partial(_combine_kernel, cap, b, n_scatter),
        grid_spec=pltpu.PrefetchScalarGridSpec(
            num_scalar_prefetch=1,
            grid=(b, n_scatter + n // FINAL_TILE),
            in_specs=[pl.BlockSpec((SCATTER_EXPERTS, cap * SLABS, LANES),
                                   lambda i, k, tok: (jnp.minimum(k, n_scatter - 1), i, 0)),
                      pl.BlockSpec((1, FINAL_TILE, D_MODEL), rows),
                      pl.BlockSpec((1, 1, D_MODEL), lambda i, k, tok: (i, 0, 0)),
                      pl.BlockSpec((1, D_MODEL), lambda i, k, tok: (0, 0))],
            out_specs=pl.BlockSpec((1, FINAL_TILE, D_MODEL), rows),
            scratch_shapes=[pltpu.VMEM((n * SLABS, LANES), F32)]),
        out_shape=jax.ShapeDtypeStruct((b, n, D_MODEL), F32),
        compiler_params=_params("arbitrary", "arbitrary"),
        name="combine",
    )(tok, y, xn, gt2, g_final)


def _rope_tables(n):
    pos = jnp.arange(n)
    n_freq = HEAD_DIM // 4
    inv = ROPE_THETA ** (-jnp.arange(n_freq, dtype=F32) / n_freq)
    ang = jnp.concatenate([(pos // GRID_W)[:, None].astype(F32) * inv,
                           (pos % GRID_W)[:, None].astype(F32) * inv], axis=-1)
    cos, sin = jnp.cos(ang), jnp.sin(ang)
    cos_t = jnp.tile(jnp.concatenate([cos, cos], axis=-1), (1, N_Q_HEADS))
    sin_t = jnp.tile(jnp.concatenate([-sin, sin], axis=-1), (1, N_Q_HEADS))
    return cos_t, sin_t


def _block_diag_ones(width, block):
    i = jnp.arange(width) // block
    return (i[:, None] == i[None, :]).astype(BF16)


def kernel(x, c, ctx, c_ctx, w_mod, b_mod, g_mix, g_ffn, w_in, q_gain, k_gain, v_gain, w_s, b_s,
           w_out, w_router, w1, w3, w2, g_final):
    assert w_mod.shape[0] == 1, "single-layer kernel"
    b, n, d = x.shape
    cap = CAPACITY_FACTOR * n // N_EXPERTS

    rows = -(-(b + 1) // 8) * 8
    cvecs = jnp.zeros((rows, d), F32).at[:b].set(c).at[b].set(c_ctx)
    mods = _adaln(cvecs, w_mod[0], b_mod[0])
    sh1, sc1, gt1, sh2, sc2, gt2 = [m[:b, None, :] for m in jnp.split(mods, 6, axis=-1)]
    csh1, csc1 = mods[b:b + 1, :d], mods[b:b + 1, d:2 * d]

    cos_t, sin_t = _rope_tables(n)
    bd = _block_diag_ones(256, HEAD_DIM)
    w_in_b = w_in[0].astype(BF16)
    g_mix2 = g_mix[0][None, :]
    qg = jnp.tile(q_gain[0], N_Q_HEADS)[None, :]
    kg = jnp.tile(k_gain[0], N_KV_HEADS)[None, :]

    q, kl, vl, u, vvn = _inproj(x, sh1, sc1, g_mix2, w_in_b, bd, qg, kg, v_gain[0][None, :],
                                cos_t, sin_t, tile=512)
    kc, vc = _ctxkv(ctx, csh1, csc1, g_mix2, w_in_b, bd, kg)
    vt_all = jnp.swapaxes(jnp.concatenate([vl, vc], axis=1), 1, 2)
    score_bound = (BOUND_MARGIN * HEAD_DIM * Q_SCALE) * jnp.max(jnp.abs(q_gain[0])) * jnp.max(jnp.abs(k_gain[0]))
    o = _attn(q, jnp.concatenate([kl, kc], axis=1), vt_all, score_bound, tile=512)

    bs = jnp.repeat(b_s[0].T, GM_GROUP_DIM, axis=1)
    xn, h2, aff_t = _mixout(o, u, vvn, x, w_s[0].astype(BF16), bs, w_out[0].astype(BF16), gt1,
                            g_ffn[0][None, :], sh2, sc2, w_router[0].T.astype(BF16), tile=512)

    tri = (jnp.arange(128)[:, None] <= jnp.arange(128)[None, :]).astype(BF16)
    pos_t = _route(aff_t, tri, cap, rows=64)
    xe, gate, tok = _gather(pos_t, aff_t, h2, cap)
    y = _ffn(xe, gate, w1[0], w3[0], w2[0], tile=512)
    return _combine(y, tok.reshape(-1), xn, gt2, g_final[None, :], cap)
```

```python
import functools
import math

import jax
import jax.numpy as jnp
from jax import lax
from jax.experimental import pallas as pl
from jax.experimental.pallas import tpu as pltpu

D_MODEL = 1024
CTX_LEN = 256
GRID_W = 64
HEAD_DIM = 64
N_Q_HEADS = 8
N_KV_HEADS = 2
Q_PER_KV = N_Q_HEADS // N_KV_HEADS
D_ATTN = N_Q_HEADS * HEAD_DIM
D_KV = N_KV_HEADS * HEAD_DIM
N_GM_GROUPS = 8
GM_GROUP_DIM = 64
D_GM = N_GM_GROUPS * GM_GROUP_DIM
CHUNK = 128
D_IN = D_ATTN + 2 * D_KV + 2 * D_GM
ROPE_THETA = 10000.0
N_EXPERTS = 16
CAPACITY_FACTOR = 2
EXPERT_FF = 2048
EPS = 1e-6

F32 = jnp.float32
BF16 = jnp.bfloat16

VMEM_LIMIT_BYTES = 52 * 1024 * 1024
SUB_ROWS = 256
LANES = 128
SLABS = D_MODEL // LANES

K_OFF = D_ATTN
V_OFF = D_ATTN + D_KV
GM_OFF = D_ATTN + 2 * D_KV

Q_SCALE = math.log2(math.e) * HEAD_DIM ** -0.5


def _params(*sem):
    return pltpu.CompilerParams(dimension_semantics=sem, vmem_limit_bytes=VMEM_LIMIT_BYTES)


def _dot(a, b):
    return jnp.dot(a, b, preferred_element_type=F32)


def _dot_nt(a, b):
    return lax.dot_general(a, b, (((1,), (1,)), ((), ())), preferred_element_type=F32)


def _rms(v):
    return lax.rsqrt(jnp.sum(v * v, axis=-1, keepdims=True) * (1.0 / v.shape[-1]) + EPS)


def _modulate(x, g, shift, scale):
    return ((x * _rms(x)) * g) * (1.0 + scale) + shift


def _seg_sum(v, bd):
    w = bd.shape[0]
    outs = []
    for j in range(v.shape[1] // w):
        c = v[:, j * w:(j + 1) * w]
        hi = c.astype(BF16)
        lo = (c - hi.astype(F32)).astype(BF16)
        outs.append(_dot(hi, bd) + _dot(lo, bd))
    return outs[0] if len(outs) == 1 else jnp.concatenate(outs, axis=-1)


def _head_norm(z, bd, gain):
    ss = _seg_sum(z * z, bd)
    return (z * lax.rsqrt(ss * (1.0 / HEAD_DIM) + EPS)) * gain


def _rope(v, cos, sin_signed):
    w = v.shape[-1]
    lane = lax.broadcasted_iota(jnp.int32, v.shape, 1)
    partner = jnp.where((lane & (HEAD_DIM // 2)) == 0,
                        pltpu.roll(v, w - HEAD_DIM // 2, axis=1),
                        pltpu.roll(v, HEAD_DIM // 2, axis=1))
    return v * cos + partner * sin_signed


def _gelu(z):
    return 0.5 * z * (1.0 + lax.erf(z * math.sqrt(0.5)))


def _adaln_kernel(c_ref, w_ref, b_ref, o_ref):
    c = c_ref[...]
    s = c / (1.0 + jnp.exp(-c))
    o_ref[...] = _dot(s.astype(BF16), w_ref[...].astype(BF16)) + b_ref[...]


def _adaln(cvecs, w_mod, b_mod):
    rows = cvecs.shape[0]
    n_out = w_mod.shape[1]
    tn = 768
    return pl.pallas_call(
        _adaln_kernel,
        grid=(n_out // tn,),
        in_specs=[pl.BlockSpec((rows, D_MODEL), lambda j: (0, 0)),
                  pl.BlockSpec((D_MODEL, tn), lambda j: (0, j)),
                  pl.BlockSpec((1, tn), lambda j: (0, j))],
        out_specs=pl.BlockSpec((rows, tn), lambda j: (0, j)),
        out_shape=jax.ShapeDtypeStruct((rows, n_out), F32),
        compiler_params=_params("arbitrary"),
        name="adaln",
    )(cvecs, w_mod, b_mod.reshape(1, n_out))


def _inproj_kernel(x_ref, sh_ref, sc_ref, g_ref, w_ref, bd_ref, qg_ref, kg_ref, vg_ref,
                   cos_ref, sin_ref, q_ref, k_ref, v_ref, u_ref, vv_ref):
    bd = bd_ref[...]
    for r0 in range(0, x_ref.shape[1], SUB_ROWS):
        rows = slice(r0, r0 + SUB_ROWS)
        h = _modulate(x_ref[0, rows, :], g_ref[...], sh_ref[0], sc_ref[0])
        z = _dot(h.astype(BF16), w_ref[...])
        cos = cos_ref[rows, :]
        sin = sin_ref[rows, :]

        q = _rope(_head_norm(z[:, :D_ATTN], bd, qg_ref[...]), cos, sin)
        q_ref[0, rows, :] = (q * Q_SCALE).astype(BF16)

        k = _head_norm(z[:, K_OFF:K_OFF + D_KV], bd[:D_KV, :D_KV], kg_ref[...])
        k_ref[0, rows, :] = _rope(k, cos[:, :D_KV], sin[:, :D_KV]).astype(BF16)
        v_ref[0, rows, :] = z[:, V_OFF:V_OFF + D_KV].astype(BF16)

        gz = _gelu(z[:, GM_OFF:])
        u_ref[0, rows, :] = gz[:, :D_GM]
        vv = gz[:, D_GM:]
        vv_ref[0, rows, :] = ((vv * _rms(vv)) * vg_ref[...]).astype(BF16)


def _inproj(x, sh1, sc1, g_mix, w_in, bd, qg, kg, vg, cos_t, sin_t, tile):
    b, n, _ = x.shape
    row = lambda i, j: (j, i, 0)
    per_b = lambda i, j: (j, 0, 0)
    const = lambda i, j: (0, 0)
    tab = lambda i, j: (i, 0)
    return pl.pallas_call(
        _inproj_kernel,
        grid=(n // tile, b),
        in_specs=[pl.BlockSpec((1, tile, D_MODEL), row),
                  pl.BlockSpec((1, 1, D_MODEL), per_b),
                  pl.BlockSpec((1, 1, D_MODEL), per_b),
                  pl.BlockSpec((1, D_MODEL), const),
                  pl.BlockSpec((D_MODEL, D_IN), const),
                  pl.BlockSpec(bd.shape, const),
                  pl.BlockSpec((1, D_ATTN), const),
                  pl.BlockSpec((1, D_KV), const),
                  pl.BlockSpec((1, D_GM), const),
                  pl.BlockSpec((tile, D_ATTN), tab),
                  pl.BlockSpec((tile, D_ATTN), tab)],
        out_specs=[pl.BlockSpec((1, tile, D_ATTN), row),
                   pl.BlockSpec((1, tile, D_KV), row),
                   pl.BlockSpec((1, tile, D_KV), row),
                   pl.BlockSpec((1, tile, D_GM), row),
                   pl.BlockSpec((1, tile, D_GM), row)],
        out_shape=[jax.ShapeDtypeStruct((b, n, D_ATTN), BF16),
                   jax.ShapeDtypeStruct((b, n, D_KV), BF16),
                   jax.ShapeDtypeStruct((b, n, D_KV), BF16),
                   jax.ShapeDtypeStruct((b, n, D_GM), F32),
                   jax.ShapeDtypeStruct((b, n, D_GM), BF16)],
        compiler_params=_params("arbitrary", "arbitrary"),
        name="inproj",
    )(x, sh1, sc1, g_mix, w_in, bd, qg, kg, vg, cos_t, sin_t)


def _ctxkv_kernel(x_ref, sh_ref, sc_ref, g_ref, w_ref, bd_ref, kg_ref, k_ref, v_ref):
    h = _modulate(x_ref[0], g_ref[...], sh_ref[...], sc_ref[...])
    z = _dot(h.astype(BF16), w_ref[:, K_OFF:K_OFF + 2 * D_KV])
    k_ref[0] = _head_norm(z[:, :D_KV], bd_ref[:D_KV, :D_KV], kg_ref[...]).astype(BF16)
    v_ref[0] = z[:, D_KV:].astype(BF16)


def _ctxkv(ctx, csh1, csc1, g_mix, w_in, bd, kg):
    b, n, _ = ctx.shape
    const = lambda i: (0, 0)
    row = lambda i: (i, 0, 0)
    return pl.pallas_call(
        _ctxkv_kernel,
        grid=(b,),
        in_specs=[pl.BlockSpec((1, n, D_MODEL), row),
                  pl.BlockSpec((1, D_MODEL), const),
                  pl.BlockSpec((1, D_MODEL), const),
                  pl.BlockSpec((1, D_MODEL), const),
                  pl.BlockSpec((D_MODEL, D_IN), const),
                  pl.BlockSpec(bd.shape, const),
                  pl.BlockSpec((1, D_KV), const)],
        out_specs=[pl.BlockSpec((1, n, D_KV), row),
                   pl.BlockSpec((1, n, D_KV), row)],
        out_shape=[jax.ShapeDtypeStruct((b, n, D_KV), BF16),
                   jax.ShapeDtypeStruct((b, n, D_KV), BF16)],
        compiler_params=_params("arbitrary"),
        name="ctxkv",
    )(ctx, csh1, csc1, g_mix, w_in, bd, kg)


ATTN_LOOKAHEAD = 4
BOUND_MARGIN = 1.02
MAX_SAFE_SCORE_BOUND = 60.0


def _attn_kernel(q_ref, k_ref, vt_ref, o_ref, st_ref):
    q = q_ref[0]
    k = k_ref[0]
    vt = vt_ref[0]
    def scores(h):
        kv = h // Q_PER_KV
        qh = q[:, h * HEAD_DIM:(h + 1) * HEAD_DIM]
        kh = k[:, kv * HEAD_DIM:(kv + 1) * HEAD_DIM]
        st_ref[h] = _dot_nt(kh, qh)

    for h in range(ATTN_LOOKAHEAD):
        scores(h)
    outs = []
    for h in range(N_Q_HEADS):
        if h + ATTN_LOOKAHEAD < N_Q_HEADS:
            scores(h + ATTN_LOOKAHEAD)
        kv = h // Q_PER_KV
        st = st_ref[h]
        pt = jnp.exp2(st - jnp.max(st, axis=0, keepdims=True))
        l = jnp.sum(pt, axis=0, keepdims=True)
        outs.append(_dot(vt[kv * HEAD_DIM:(kv + 1) * HEAD_DIM, :], pt.astype(BF16)) / l)
    o_ref[0] = jnp.concatenate(outs, axis=0).T.astype(BF16)


def _attn_bounded_kernel(shift_ref, q_ref, k_ref, vt_ref, o_ref, pt_ref):
    q = q_ref[0]
    k = k_ref[0]
    vt = vt_ref[0]
    shift = shift_ref[...]
    sums = [None] * N_Q_HEADS

    def probs(h):
        kv = h // Q_PER_KV
        qh = q[:, h * HEAD_DIM:(h + 1) * HEAD_DIM]
        kh = k[:, kv * HEAD_DIM:(kv + 1) * HEAD_DIM]
        pt = jnp.exp2(_dot_nt(kh, qh) - shift)
        sums[h] = jnp.sum(pt, axis=0, keepdims=True)
        pt_ref[h] = pt.astype(BF16)

    for h in range(ATTN_LOOKAHEAD):
        probs(h)
    outs = []
    for h in range(N_Q_HEADS):
        if h + ATTN_LOOKAHEAD < N_Q_HEADS:
            probs(h + ATTN_LOOKAHEAD)
        kv = h // Q_PER_KV
        outs.append(_dot(vt[kv * HEAD_DIM:(kv + 1) * HEAD_DIM, :], pt_ref[h]) / sums[h])
    o_ref[0] = jnp.concatenate(outs, axis=0).T.astype(BF16)


def _attn(q, k_all, vt_all, score_bound, tile):
    b, n, _ = q.shape
    nk = k_all.shape[1]
    specs = dict(
        grid=(b, n // tile),
        out_specs=pl.BlockSpec((1, tile, D_ATTN), lambda i, j: (i, j, 0)),
        out_shape=jax.ShapeDtypeStruct((b, n, D_ATTN), BF16),
        compiler_params=_params("arbitrary", "arbitrary"))
    qkv_specs = [pl.BlockSpec((1, tile, D_ATTN), lambda i, j: (i, j, 0)),
                 pl.BlockSpec((1, nk, D_KV), lambda i, j: (i, 0, 0)),
                 pl.BlockSpec((1, D_KV, nk), lambda i, j: (i, 0, 0))]

    def exact_max(q, k_all, vt_all, score_bound):
        return pl.pallas_call(
            _attn_kernel, in_specs=qkv_specs,
            scratch_shapes=[pltpu.VMEM((N_Q_HEADS, nk, tile), F32)], name="attn", **specs,
        )(q, k_all, vt_all)

    def bounded(q, k_all, vt_all, score_bound):
        return pl.pallas_call(
            _attn_bounded_kernel, in_specs=[pl.BlockSpec((1, tile), lambda i, j: (0, 0))] + qkv_specs,
            scratch_shapes=[pltpu.VMEM((N_Q_HEADS, nk, tile), BF16)], name="attn_bounded", **specs,
        )(jnp.full((1, tile), score_bound, F32), q, k_all, vt_all)

    return lax.cond(score_bound <= MAX_SAFE_SCORE_BOUND, bounded, exact_max,
                    q, k_all, vt_all, score_bound)


def _mixout_kernel(o_ref, u_ref, vv_ref, x_ref, ws_ref, bs_ref, wout_ref, gt_ref, g_ref,
                   sh_ref, sc_ref, wr_ref, xn_ref, h_ref, aff_ref):
    tile = o_ref.shape[1]
    seg = lax.broadcasted_iota(jnp.int32, (CHUNK, D_GM), 1) // GM_GROUP_DIM
    bs = bs_ref[...]
    gms = []
    for c in range(tile // CHUNK):
        vc = vv_ref[0, c * CHUNK:(c + 1) * CHUNK, :]
        mixed = jnp.zeros((CHUNK, D_GM), F32)
        for g in range(N_GM_GROUPS):
            mixed = jnp.where(seg == g, _dot(ws_ref[g], vc), mixed)
        gms.append(u_ref[0, c * CHUNK:(c + 1) * CHUNK, :] * (mixed + bs))
    gm = jnp.concatenate(gms, axis=0).astype(BF16)

    proj = _dot(o_ref[0], wout_ref[:D_ATTN, :]) + _dot(gm, wout_ref[D_ATTN:, :])
    xn = x_ref[0] + gt_ref[0] * proj
    xn_ref[0] = xn

    h = _modulate(xn, g_ref[...], sh_ref[0], sc_ref[0]).astype(BF16)
    h_ref[0] = h
    logits = _dot_nt(wr_ref[...], h)
    e = jnp.exp(logits - jnp.max(logits, axis=0, keepdims=True))
    aff_ref[0] = e / jnp.sum(e, axis=0, keepdims=True)


def _mixout(o, u, vvn, x, w_s, bs, w_out, gt1, g_ffn, sh2, sc2, w_rt, tile):
    b, n, _ = x.shape
    row = lambda i, j: (i, j, 0)
    per_b = lambda i, j: (i, 0, 0)
    const2 = lambda i, j: (0, 0)
    return pl.pallas_call(
        _mixout_kernel,
        grid=(b, n // tile),
        in_specs=[pl.BlockSpec((1, tile, D_ATTN), row),
                  pl.BlockSpec((1, tile, D_GM), row),
                  pl.BlockSpec((1, tile, D_GM), row),
                  pl.BlockSpec((1, tile, D_MODEL), row),
                  pl.BlockSpec((N_GM_GROUPS, CHUNK, CHUNK), lambda i, j: (0, 0, 0)),
                  pl.BlockSpec((CHUNK, D_GM), const2),
                  pl.BlockSpec((D_ATTN + D_GM, D_MODEL), const2),
                  pl.BlockSpec((1, 1, D_MODEL), per_b),
                  pl.BlockSpec((1, D_MODEL), const2),
                  pl.BlockSpec((1, 1, D_MODEL), per_b),
                  pl.BlockSpec((1, 1, D_MODEL), per_b),
                  pl.BlockSpec((N_EXPERTS, D_MODEL), const2)],
        out_specs=[pl.BlockSpec((1, tile, D_MODEL), row),
                   pl.BlockSpec((1, tile, D_MODEL), row),
                   pl.BlockSpec((1, N_EXPERTS, tile), lambda i, j: (i, 0, j))],
        out_shape=[jax.ShapeDtypeStruct((b, n, D_MODEL), F32),
                   jax.ShapeDtypeStruct((b, n, D_MODEL), BF16),
                   jax.ShapeDtypeStruct((b, N_EXPERTS, n), F32)],
        compiler_params=_params("arbitrary", "arbitrary"),
        name="mixout",
    )(o, u, vvn, x, w_s, bs, w_out, gt1, g_ffn, sh2, sc2, w_rt)


def _cumsum_lanes(m, tri):
    off = jnp.zeros((m.shape[0], 1), F32)
    outs = []
    for blk in range(m.shape[1] // 128):
        mb = m[:, blk * 128:(blk + 1) * 128]
        outs.append(_dot(mb.astype(BF16), tri) + off)
        off = off + jnp.sum(mb, axis=-1, keepdims=True)
    return jnp.concatenate(outs, axis=-1)


REFINE_STEPS = 32


def _route_kernel(cap, aff_ref, tri_ref, pos_ref):
    a = aff_ref[...]
    capf = float(cap)

    def count_ge(t):
        return jnp.sum(jnp.where(a >= t, 1.0, 0.0), axis=-1, keepdims=True)

    def bit_step(i, prefix):
        cand = prefix | jnp.left_shift(jnp.int32(1), 30 - i)
        return jnp.where(count_ge(pltpu.bitcast(cand, F32)) >= capf, cand, prefix)

    prefix = lax.fori_loop(0, 31, bit_step, jnp.zeros((a.shape[0], 1), jnp.int32))
    lo = pltpu.bitcast(prefix, F32)
    hi = pltpu.bitcast(prefix + 1, F32)

    def refine(_, carry):
        lo, hi = carry
        mid = 0.5 * lo + 0.5 * hi
        ok = count_ge(mid) >= capf
        return jnp.where(ok, mid, lo), jnp.where(ok, hi, mid)

    lo, hi = lax.fori_loop(0, REFINE_STEPS, refine, (lo, hi))
    thr = jnp.min(jnp.where(a >= lo, a, jnp.inf), axis=-1, keepdims=True)

    gt = jnp.where(a > thr, 1.0, 0.0)
    eq = jnp.where(a == thr, 1.0, 0.0)
    need = capf - jnp.sum(gt, axis=-1, keepdims=True)
    tri = tri_ref[...]
    sel = gt + eq * jnp.where(_cumsum_lanes(eq, tri) <= need, 1.0, 0.0)
    slot = _cumsum_lanes(sel, tri) - 1.0
    pos_ref[...] = jnp.where(sel > 0.0, slot, -1.0).astype(jnp.int32)


def _route(aff_t, tri, cap, rows):
    b, e, n = aff_t.shape
    pos = pl.pallas_call(
        functools.partial(_route_kernel, cap),
        grid=(b * e // rows,),
        in_specs=[pl.BlockSpec((rows, n), lambda i: (i, 0)),
                  pl.BlockSpec((128, 128), lambda i: (0, 0))],
        out_specs=pl.BlockSpec((rows, n), lambda i: (i, 0)),
        out_shape=jax.ShapeDtypeStruct((b * e, n), jnp.int32),
        compiler_params=_params("arbitrary"),
        name="route",
    )(aff_t.reshape(b * e, n), tri)
    return pos.reshape(b, e, n)


GATHER_EXPERTS = 4


def _gather_kernel(cap, pos_ref, aff_ref, h_ref, xe_ref, gate_ref, tok_ref):
    n = pos_ref.shape[-1]
    slot = lax.broadcasted_iota(jnp.int32, (cap, n), 0)
    tok = (lax.broadcasted_iota(jnp.int32, (cap, n), 1) * SLABS).astype(F32)
    for x in range(GATHER_EXPERTS):
        hit = pos_ref[0, x] == slot
        xe_ref[x] = _dot(jnp.where(hit, 1.0, 0.0).astype(BF16), h_ref[0]).astype(BF16)
        gate_ref[x] = jnp.sum(jnp.where(hit, aff_ref[0, x], 0.0), axis=-1, keepdims=True)
        tok_ref[x] = jnp.sum(jnp.where(hit, tok, 0.0), axis=-1, keepdims=True).astype(jnp.int32)


def _gather(pos_t, aff_t, h2, cap):
    b, e, n = pos_t.shape
    row = lambda i, j: (i, j, 0, 0)
    slots = lambda i, j: (j, i, 0)
    return pl.pallas_call(
        functools.partial(_gather_kernel, cap),
        grid=(b, e // GATHER_EXPERTS),
        in_specs=[pl.BlockSpec((1, GATHER_EXPERTS, 1, n), row),
                  pl.BlockSpec((1, GATHER_EXPERTS, 1, n), row),
                  pl.BlockSpec((1, n, D_MODEL), lambda i, j: (i, 0, 0))],
        out_specs=[pl.BlockSpec((GATHER_EXPERTS, cap, D_MODEL), slots),
                   pl.BlockSpec((GATHER_EXPERTS, cap, 1), slots),
                   pl.BlockSpec((GATHER_EXPERTS, cap, 1), slots)],
        out_shape=[jax.ShapeDtypeStruct((e, b * cap, D_MODEL), BF16),
                   jax.ShapeDtypeStruct((e, b * cap, 1), F32),
                   jax.ShapeDtypeStruct((e, b * cap, 1), jnp.int32)],
        compiler_params=_params("arbitrary", "arbitrary"),
        name="gather",
    )(pos_t.reshape(b, e, 1, n), aff_t.reshape(b, e, 1, n), h2)


FF_CHUNK = 512


def _ffn_kernel(n_experts, xe_ref, gate_ref, w1_ref, w3_ref, w2_ref, y_ref, w1b, w3b, w2b):
    i = pl.program_id(0)
    j = pl.program_id(1)
    rows13 = w1_ref.shape[1]
    rows2 = w2_ref.shape[1]
    tile = xe_ref.shape[1]

    @pl.when(i < n_experts)
    def _():
        slot = i % 2
        w1b[slot, pl.ds(j * rows13, rows13), :] = w1_ref[0].astype(BF16)
        w3b[slot, pl.ds(j * rows13, rows13), :] = w3_ref[0].astype(BF16)
        w2b[slot, pl.ds(j * rows2, rows2), :] = w2_ref[0].astype(BF16)

    @pl.when(i == 0)
    def _():
        y_ref[...] = jnp.zeros_like(y_ref)

    @pl.when(i > 0)
    def _():
        slot = (i - 1) % 2
        xe = xe_ref[0]
        acc = jnp.zeros((tile, D_MODEL), F32)
        for c in range(EXPERT_FF // FF_CHUNK):
            cols = slice(c * FF_CHUNK, (c + 1) * FF_CHUNK)
            a = _dot(xe, w1b[slot, :, cols])
            g = _dot(xe, w3b[slot, :, cols])
            hid = (a / (1.0 + jnp.exp(-a))) * g
            acc = acc + _dot(hid.astype(BF16), w2b[slot, cols, :])
        y = acc * gate_ref[0]
        for s in range(SLABS):
            y_ref[0, pl.ds(s, tile, stride=SLABS), :] = y[:, s * LANES:(s + 1) * LANES]


def _ffn(xe, gate, w1, w3, w2, tile):
    e, m, _ = xe.shape
    steps = m // tile
    cur = lambda i, j: (jnp.maximum(i - 1, 0), j, 0)
    nxt = lambda i, j: (jnp.minimum(i, e - 1), j, 0)
    return pl.pallas_call(
        functools.partial(_ffn_kernel, e),
        grid=(e + 1, steps),
        in_specs=[pl.BlockSpec((1, tile, D_MODEL), cur),
                  pl.BlockSpec((1, tile, 1), cur),
                  pl.BlockSpec((1, D_MODEL // steps, EXPERT_FF), nxt),
                  pl.BlockSpec((1, D_MODEL // steps, EXPERT_FF), nxt),
                  pl.BlockSpec((1, EXPERT_FF // steps, D_MODEL), nxt)],
        out_specs=pl.BlockSpec((1, tile * SLABS, LANES), lambda i, j: (jnp.where(i == 0, e, i - 1), j, 0)),
        out_shape=jax.ShapeDtypeStruct((e + 1, m * SLABS, LANES), F32),
        scratch_shapes=[pltpu.VMEM((2, D_MODEL, EXPERT_FF), BF16),
                        pltpu.VMEM((2, D_MODEL, EXPERT_FF), BF16),
                        pltpu.VMEM((2, EXPERT_FF, D_MODEL), BF16)],
        compiler_params=_params("arbitrary", "arbitrary"),
        name="ffn",
    )(xe, gate, w1, w3, w2)


SCATTER_UNROLL = 8
SCATTER_EXPERTS = 4
FINAL_TILE = 512


def _combine_kernel(cap, n_batch, n_scatter, tok_ref, y_ref, xn_ref, gt_ref, g_ref, o_ref, acc_ref):
    b = pl.program_id(0)
    k = pl.program_id(1)

    @pl.when(k == 0)
    def _():
        acc_ref[...] = jnp.zeros_like(acc_ref)

    def scatter_expert(x):
        base = ((k * SCATTER_EXPERTS + x) * n_batch + b) * cap
        for g in range(cap // SCATTER_UNROLL):
            rows, vals = [], []
            for u in range(SCATTER_UNROLL):
                r = g * SCATTER_UNROLL + u
                row = pl.ds(pl.multiple_of(tok_ref[base + r], SLABS), SLABS)
                rows.append(row)
                vals.append(acc_ref[row, :] + y_ref[x, r * SLABS:(r + 1) * SLABS, :])
            for row, val in zip(rows, vals):
                acc_ref[row, :] = val

    @pl.when(k < n_scatter)
    def _():
        for x in range(SCATTER_EXPERTS):
            scatter_expert(x)

    @pl.when(k >= n_scatter)
    def _():
        start = (k - n_scatter) * (FINAL_TILE * SLABS)
        moe = jnp.concatenate(
            [acc_ref[pl.ds(start + s, FINAL_TILE, stride=SLABS), :] for s in range(SLABS)],
            axis=-1)
        z = xn_ref[0] + gt_ref[0] * moe
        o_ref[0] = (z * _rms(z)) * g_ref[...]


def _combine(y, tok, xn, gt2, g_final, cap):
    b, n, _ = xn.shape
    e = y.shape[0] - 1
    n_scatter = e // SCATTER_EXPERTS
    rows = lambda i, k, tok: (i, jnp.maximum(k - n_scatter, 0), 0)
    return pl.pallas_call(
        functools.partial(_combine_kernel, cap, b, n_scatter),
        grid_spec=pltpu.PrefetchScalarGridSpec(
            num_scalar_prefetch=1,
            grid=(b, n_scatter + n // FINAL_TILE),
            in_specs=[pl.BlockSpec((SCATTER_EXPERTS, cap * SLABS, LANES),
                                   lambda i, k, tok: (jnp.minimum(k, n_scatter - 1), i, 0)),
                      pl.BlockSpec((1, FINAL_TILE, D_MODEL), rows),
                      pl.BlockSpec((1, 1, D_MODEL), lambda i, k, tok: (i, 0, 0)),
                      pl.BlockSpec((1, D_MODEL), lambda i, k, tok: (0, 0))],
            out_specs=pl.BlockSpec((1, FINAL_TILE, D_MODEL), rows),
            scratch_shapes=[pltpu.VMEM((n * SLABS, LANES), F32)]),
        out_shape=jax.ShapeDtypeStruct((b, n, D_MODEL), F32),
        compiler_params=_params("arbitrary", "arbitrary"),
        name="combine",
    )(tok, y, xn, gt2, g_final)


def _rope_tables(n):
    pos = jnp.arange(n)
    n_freq = HEAD_DIM // 4
    inv = ROPE_THETA ** (-jnp.arange(n_freq, dtype=F32) / n_freq)
    ang = jnp.concatenate([(pos // GRID_W)[:, None].astype(F32) * inv,
                           (pos % GRID_W)[:, None].astype(F32) * inv], axis=-1)
    cos, sin = jnp.cos(ang), jnp.sin(ang)
    cos_t = jnp.tile(jnp.concatenate([cos, cos], axis=-1), (1, N_Q_HEADS))
    sin_t = jnp.tile(jnp.concatenate([-sin, sin], axis=-1), (1, N_Q_HEADS))
    return cos_t, sin_t


def _block_diag_ones(width, block):
    i = jnp.arange(width) // block
    return (i[:, None] == i[None, :]).astype(BF16)


def kernel(x, c, ctx, c_ctx, w_mod, b_mod, g_mix, g_ffn, w_in, q_gain, k_gain, v_gain, w_s, b_s,
           w_out, w_router, w1, w3, w2, g_final):
    assert w_mod.shape[0] == 1, "single-layer kernel"
    b, n, d = x.shape
    cap = CAPACITY_FACTOR * n // N_EXPERTS

    rows = -(-(b + 1) // 8) * 8
    cvecs = jnp.zeros((rows, d), F32).at[:b].set(c).at[b].set(c_ctx)
    mods = _adaln(cvecs, w_mod[0], b_mod[0])
    sh1, sc1, gt1, sh2, sc2, gt2 = [m[:b, None, :] for m in jnp.split(mods, 6, axis=-1)]
    csh1, csc1 = mods[b:b + 1, :d], mods[b:b + 1, d:2 * d]

    cos_t, sin_t = _rope_tables(n)
    bd = _block_diag_ones(256, HEAD_DIM)
    w_in_b = w_in[0].astype(BF16)
    g_mix2 = g_mix[0][None, :]
    qg = jnp.tile(q_gain[0], N_Q_HEADS)[None, :]
    kg = jnp.tile(k_gain[0], N_KV_HEADS)[None, :]

    q, kl, vl, u, vvn = _inproj(x, sh1, sc1, g_mix2, w_in_b, bd, qg, kg, v_gain[0][None, :],
                                cos_t, sin_t, tile=512)
    kc, vc = _ctxkv(ctx, csh1, csc1, g_mix2, w_in_b, bd, kg)
    vt_all = jnp.swapaxes(jnp.concatenate([vl, vc], axis=1), 1, 2)
    score_bound = (BOUND_MARGIN * HEAD_DIM * Q_SCALE) * jnp.max(jnp.abs(q_gain[0])) * jnp.max(jnp.abs(k_gain[0]))
    o = _attn(q, jnp.concatenate([kl, kc], axis=1), vt_all, score_bound, tile=512)

    bs = jnp.repeat(b_s[0].T, GM_GROUP_DIM, axis=1)
    xn, h2, aff_t = _mixout(o, u, vvn, x, w_s[0].astype(BF16), bs, w_out[0].astype(BF16), gt1,
                            g_ffn[0][None, :], sh2, sc2, w_router[0].T.astype(BF16), tile=512)

    tri = (jnp.arange(128)[:, None] <= jnp.arange(128)[None, :]).astype(BF16)
    pos_t = _route(aff_t, tri, cap, rows=64)
    xe, gate, tok = _gather(pos_t, aff_t, h2, cap)
    y = _ffn(xe, gate, w1[0], w3[0], w2[0], tile=512)
    return _combine(y, tok.reshape(-1), xn, gt2, g_final[None, :], cap)
```

```python
import functools
import math

import jax
import jax.numpy as jnp
from jax import lax
from jax.experimental import pallas as pl
from jax.experimental.pallas import tpu as pltpu

D_MODEL = 1024
CTX_LEN = 256
GRID_W = 64
HEAD_DIM = 64
N_Q_HEADS = 8
N_KV_HEADS = 2
Q_PER_KV = N_Q_HEADS // N_KV_HEADS
D_ATTN = N_Q_HEADS * HEAD_DIM
D_KV = N_KV_HEADS * HEAD_DIM
N_GM_GROUPS = 8
GM_GROUP_DIM = 64
D_GM = N_GM_GROUPS * GM_GROUP_DIM
CHUNK = 128
D_IN = D_ATTN + 2 * D_KV + 2 * D_GM
ROPE_THETA = 10000.0
N_EXPERTS = 16
CAPACITY_FACTOR = 2
EXPERT_FF = 2048
EPS = 1e-6

F32 = jnp.float32
BF16 = jnp.bfloat16

VMEM_LIMIT_BYTES = 52 * 1024 * 1024
SUB_ROWS = 256
LANES = 128
SLABS = D_MODEL // LANES

K_OFF = D_ATTN
V_OFF = D_ATTN + D_KV
GM_OFF = D_ATTN + 2 * D_KV

Q_SCALE = math.log2(math.e) * HEAD_DIM ** -0.5


def _params(*sem):
    return pltpu.CompilerParams(dimension_semantics=sem, vmem_limit_bytes=VMEM_LIMIT_BYTES)


def _dot(a, b):
    return jnp.dot(a, b, preferred_element_type=F32)


def _dot_nt(a, b):
    return lax.dot_general(a, b, (((1,), (1,)), ((), ())), preferred_element_type=F32)


def _rms(v):
    return lax.rsqrt(jnp.sum(v * v, axis=-1, keepdims=True) * (1.0 / v.shape[-1]) + EPS)


def _modulate(x, g, shift, scale):
    return ((x * _rms(x)) * g) * (1.0 + scale) + shift


def _seg_sum(v, bd):
    w = bd.shape[0]
    outs = []
    for j in range(v.shape[1] // w):
        c = v[:, j * w:(j + 1) * w]
        hi = c.astype(BF16)
        lo = (c - hi.astype(F32)).astype(BF16)
        outs.append(_dot(hi, bd) + _dot(lo, bd))
    return outs[0] if len(outs) == 1 else jnp.concatenate(outs, axis=-1)


def _head_norm(z, bd, gain):
    ss = _seg_sum(z * z, bd)
    return (z * lax.rsqrt(ss * (1.0 / HEAD_DIM) + EPS)) * gain


def _rope(v, cos, sin_signed):
    w = v.shape[-1]
    lane = lax.broadcasted_iota(jnp.int32, v.shape, 1)
    partner = jnp.where((lane & (HEAD_DIM // 2)) == 0,
                        pltpu.roll(v, w - HEAD_DIM // 2, axis=1),
                        pltpu.roll(v, HEAD_DIM // 2, axis=1))
    return v * cos + partner * sin_signed


def _gelu(z):
    return 0.5 * z * (1.0 + lax.erf(z * math.sqrt(0.5)))


def _adaln_kernel(c_ref, w_ref, b_ref, o_ref):
    c = c_ref[...]
    s = c / (1.0 + jnp.exp(-c))
    o_ref[...] = _dot(s.astype(BF16), w_ref[...].astype(BF16)) + b_ref[...]


def _adaln(cvecs, w_mod, b_mod):
    rows = cvecs.shape[0]
    n_out = w_mod.shape[1]
    tn = 768
    return pl.pallas_call(
        _adaln_kernel,
        grid=(n_out // tn,),
        in_specs=[pl.BlockSpec((rows, D_MODEL), lambda j: (0, 0)),
                  pl.BlockSpec((D_MODEL, tn), lambda j: (0, j)),
                  pl.BlockSpec((1, tn), lambda j: (0, j))],
        out_specs=pl.BlockSpec((rows, tn), lambda j: (0, j)),
        out_shape=jax.ShapeDtypeStruct((rows, n_out), F32),
        compiler_params=_params("arbitrary"),
        name="adaln",
    )(cvecs, w_mod, b_mod.reshape(1, n_out))


def _inproj_kernel(x_ref, sh_ref, sc_ref, g_ref, w_ref, bd_ref, qg_ref, kg_ref, vg_ref,
                   cos_ref, sin_ref, q_ref, k_ref, vt_ref, u_ref, vv_ref):
    bd = bd_ref[...]
    for r0 in range(0, x_ref.shape[1], SUB_ROWS):
        rows = slice(r0, r0 + SUB_ROWS)
        h = _modulate(x_ref[0, rows, :], g_ref[...], sh_ref[0], sc_ref[0])
        z = _dot(h.astype(BF16), w_ref[...])
        cos = cos_ref[rows, :]
        sin = sin_ref[rows, :]

        q = _rope(_head_norm(z[:, :D_ATTN], bd, qg_ref[...]), cos, sin)
        q_ref[0, rows, :] = (q * Q_SCALE).astype(BF16)

        k = _head_norm(z[:, K_OFF:K_OFF + D_KV], bd[:D_KV, :D_KV], kg_ref[...])
        k_ref[0, rows, :] = _rope(k, cos[:, :D_KV], sin[:, :D_KV]).astype(BF16)
        vt_ref[0, :, rows] = z[:, V_OFF:V_OFF + D_KV].T.astype(BF16)

        gz = _gelu(z[:, GM_OFF:])
        u_ref[0, rows, :] = gz[:, :D_GM]
        vv = gz[:, D_GM:]
        vv_ref[0, rows, :] = ((vv * _rms(vv)) * vg_ref[...]).astype(BF16)


def _inproj(x, sh1, sc1, g_mix, w_in, bd, qg, kg, vg, cos_t, sin_t, tile):
    b, n, _ = x.shape
    row = lambda i, j: (j, i, 0)
    per_b = lambda i, j: (j, 0, 0)
    const = lambda i, j: (0, 0)
    tab = lambda i, j: (i, 0)
    return pl.pallas_call(
        _inproj_kernel,
        grid=(n // tile, b),
        in_specs=[pl.BlockSpec((1, tile, D_MODEL), row),
                  pl.BlockSpec((1, 1, D_MODEL), per_b),
                  pl.BlockSpec((1, 1, D_MODEL), per_b),
                  pl.BlockSpec((1, D_MODEL), const),
                  pl.BlockSpec((D_MODEL, D_IN), const),
                  pl.BlockSpec(bd.shape, const),
                  pl.BlockSpec((1, D_ATTN), const),
                  pl.BlockSpec((1, D_KV), const),
                  pl.BlockSpec((1, D_GM), const),
                  pl.BlockSpec((tile, D_ATTN), tab),
                  pl.BlockSpec((tile, D_ATTN), tab)],
        out_specs=[pl.BlockSpec((1, tile, D_ATTN), row),
                   pl.BlockSpec((1, tile, D_KV), row),
                   pl.BlockSpec((1, D_KV, tile), lambda i, j: (j, 0, i)),
                   pl.BlockSpec((1, tile, D_GM), row),
                   pl.BlockSpec((1, tile, D_GM), row)],
        out_shape=[jax.ShapeDtypeStruct((b, n, D_ATTN), BF16),
                   jax.ShapeDtypeStruct((b, n, D_KV), BF16),
                   jax.ShapeDtypeStruct((b, D_KV, n), BF16),
                   jax.ShapeDtypeStruct((b, n, D_GM), F32),
                   jax.ShapeDtypeStruct((b, n, D_GM), BF16)],
        compiler_params=_params("arbitrary", "arbitrary"),
        name="inproj",
    )(x, sh1, sc1, g_mix, w_in, bd, qg, kg, vg, cos_t, sin_t)


def _ctxkv_kernel(x_ref, sh_ref, sc_ref, g_ref, w_ref, bd_ref, kg_ref, k_ref, vt_ref):
    h = _modulate(x_ref[0], g_ref[...], sh_ref[...], sc_ref[...])
    z = _dot(h.astype(BF16), w_ref[:, K_OFF:K_OFF + 2 * D_KV])
    k_ref[0] = _head_norm(z[:, :D_KV], bd_ref[:D_KV, :D_KV], kg_ref[...]).astype(BF16)
    vt_ref[0] = z[:, D_KV:].T.astype(BF16)


def _ctxkv(ctx, csh1, csc1, g_mix, w_in, bd, kg):
    b, n, _ = ctx.shape
    const = lambda i: (0, 0)
    row = lambda i: (i, 0, 0)
    return pl.pallas_call(
        _ctxkv_kernel,
        grid=(b,),
        in_specs=[pl.BlockSpec((1, n, D_MODEL), row),
                  pl.BlockSpec((1, D_MODEL), const),
                  pl.BlockSpec((1, D_MODEL), const),
                  pl.BlockSpec((1, D_MODEL), const),
                  pl.BlockSpec((D_MODEL, D_IN), const),
                  pl.BlockSpec(bd.shape, const),
                  pl.BlockSpec((1, D_KV), const)],
        out_specs=[pl.BlockSpec((1, n, D_KV), row),
                   pl.BlockSpec((1, D_KV, n), row)],
        out_shape=[jax.ShapeDtypeStruct((b, n, D_KV), BF16),
                   jax.ShapeDtypeStruct((b, D_KV, n), BF16)],
        compiler_params=_params("arbitrary"),
        name="ctxkv",
    )(ctx, csh1, csc1, g_mix, w_in, bd, kg)


ATTN_LOOKAHEAD = 4
BOUND_MARGIN = 1.02
MAX_SAFE_SCORE_BOUND = 60.0


def _head_slices(q_ref, kl_ref, kc_ref, h):
    kv = slice((h // Q_PER_KV) * HEAD_DIM, (h // Q_PER_KV + 1) * HEAD_DIM)
    qh = q_ref[0, :, h * HEAD_DIM:(h + 1) * HEAD_DIM]
    return qh, kl_ref[0, :, kv], kc_ref[0, :, kv]


def _weighted_values(vlt_ref, vct_ref, p_ref, h, buf, n_lat):
    kv = slice((h // Q_PER_KV) * HEAD_DIM, (h // Q_PER_KV + 1) * HEAD_DIM)
    return (_dot(vlt_ref[0, kv, :], p_ref[buf, :n_lat, :])
            + _dot(vct_ref[0, kv, :], p_ref[buf, n_lat:, :]))


def _attn_kernel(q_ref, kl_ref, kc_ref, vlt_ref, vct_ref, o_ref, st_ref, pt_ref):
    n_lat = kl_ref.shape[1]

    def scores(h):
        qh, kl, kc = _head_slices(q_ref, kl_ref, kc_ref, h)
        st_ref[h, :n_lat, :] = _dot_nt(kl, qh)
        st_ref[h, n_lat:, :] = _dot_nt(kc, qh)

    for h in range(ATTN_LOOKAHEAD):
        scores(h)
    outs = []
    for h in range(N_Q_HEADS):
        if h + ATTN_LOOKAHEAD < N_Q_HEADS:
            scores(h + ATTN_LOOKAHEAD)
        st = st_ref[h]
        pt = jnp.exp2(st - jnp.max(st, axis=0, keepdims=True))
        l = jnp.sum(pt, axis=0, keepdims=True)
        pt_ref[h % 2] = pt.astype(BF16)
        outs.append(_weighted_values(vlt_ref, vct_ref, pt_ref, h, h % 2, n_lat) / l)
    o_ref[0] = jnp.concatenate(outs, axis=0).T.astype(BF16)


def _attn_bounded_kernel(shift_ref, q_ref, kl_ref, kc_ref, vlt_ref, vct_ref, o_ref, pt_ref):
    n_lat = kl_ref.shape[1]
    shift = shift_ref[...]
    sums = [None] * N_Q_HEADS

    def probs(h):
        qh, kl, kc = _head_slices(q_ref, kl_ref, kc_ref, h)
        pl_t = jnp.exp2(_dot_nt(kl, qh) - shift)
        pc_t = jnp.exp2(_dot_nt(kc, qh) - shift)
        sums[h] = jnp.sum(pl_t, axis=0, keepdims=True) + jnp.sum(pc_t, axis=0, keepdims=True)
        pt_ref[h, :n_lat, :] = pl_t.astype(BF16)
        pt_ref[h, n_lat:, :] = pc_t.astype(BF16)

    for h in range(ATTN_LOOKAHEAD):
        probs(h)
    outs = []
    for h in range(N_Q_HEADS):
        if h + ATTN_LOOKAHEAD < N_Q_HEADS:
            probs(h + ATTN_LOOKAHEAD)
        outs.append(_weighted_values(vlt_ref, vct_ref, pt_ref, h, h, n_lat) / sums[h])
    o_ref[0] = jnp.concatenate(outs, axis=0).T.astype(BF16)


def _attn(q, kl, kc, vlt, vct, score_bound, tile):
    b, n, _ = q.shape
    n_ctx = kc.shape[1]
    nk = n + n_ctx
    specs = dict(
        grid=(b, n // tile),
        out_specs=pl.BlockSpec((1, tile, D_ATTN), lambda i, j: (i, j, 0)),
        out_shape=jax.ShapeDtypeStruct((b, n, D_ATTN), BF16),
        compiler_params=_params("arbitrary", "arbitrary"))
    per_b = lambda i, j: (i, 0, 0)
    qkv_specs = [pl.BlockSpec((1, tile, D_ATTN), lambda i, j: (i, j, 0)),
                 pl.BlockSpec((1, n, D_KV), per_b),
                 pl.BlockSpec((1, n_ctx, D_KV), per_b),
                 pl.BlockSpec((1, D_KV, n), per_b),
                 pl.BlockSpec((1, D_KV, n_ctx), per_b)]

    def exact_max(score_bound, *qkv):
        return pl.pallas_call(
            _attn_kernel, in_specs=qkv_specs,
            scratch_shapes=[pltpu.VMEM((N_Q_HEADS, nk, tile), F32), pltpu.VMEM((2, nk, tile), BF16)],
            name="attn", **specs,
        )(*qkv)

    def bounded(score_bound, *qkv):
        return pl.pallas_call(
            _attn_bounded_kernel, in_specs=[pl.BlockSpec((1, tile), lambda i, j: (0, 0))] + qkv_specs,
            scratch_shapes=[pltpu.VMEM((N_Q_HEADS, nk, tile), BF16)], name="attn_bounded", **specs,
        )(jnp.full((1, tile), score_bound, F32), *qkv)

    return lax.cond(score_bound <= MAX_SAFE_SCORE_BOUND, bounded, exact_max,
                    score_bound, q, kl, kc, vlt, vct)


def _mixout_kernel(o_ref, u_ref, vv_ref, x_ref, ws_ref, bs_ref, wout_ref, gt_ref, g_ref,
                   sh_ref, sc_ref, wr_ref, xn_ref, h_ref, aff_ref):
    tile = o_ref.shape[1]
    seg = lax.broadcasted_iota(jnp.int32, (CHUNK, D_GM), 1) // GM_GROUP_DIM
    bs = bs_ref[...]
    gms = []
    for c in range(tile // CHUNK):
        vc = vv_ref[0, c * CHUNK:(c + 1) * CHUNK, :]
        mixed = jnp.zeros((CHUNK, D_GM), F32)
        for g in range(N_GM_GROUPS):
            mixed = jnp.where(seg == g, _dot(ws_ref[g], vc), mixed)
        gms.append(u_ref[0, c * CHUNK:(c + 1) * CHUNK, :] * (mixed + bs))
    gm = jnp.concatenate(gms, axis=0).astype(BF16)

    proj = _dot(o_ref[0], wout_ref[:D_ATTN, :]) + _dot(gm, wout_ref[D_ATTN:, :])
    xn = x_ref[0] + gt_ref[0] * proj
    xn_ref[0] = xn

    h = _modulate(xn, g_ref[...], sh_ref[0], sc_ref[0]).astype(BF16)
    h_ref[0] = h
    logits = _dot_nt(wr_ref[...], h)
    e = jnp.exp(logits - jnp.max(logits, axis=0, keepdims=True))
    aff_ref[0] = e / jnp.sum(e, axis=0, keepdims=True)


def _mixout(o, u, vvn, x, w_s, bs, w_out, gt1, g_ffn, sh2, sc2, w_rt, tile):
    b, n, _ = x.shape
    row = lambda i, j: (i, j, 0)
    per_b = lambda i, j: (i, 0, 0)
    const2 = lambda i, j: (0, 0)
    return pl.pallas_call(
        _mixout_kernel,
        grid=(b, n // tile),
        in_specs=[pl.BlockSpec((1, tile, D_ATTN), row),
                  pl.BlockSpec((1, tile, D_GM), row),
                  pl.BlockSpec((1, tile, D_GM), row),
                  pl.BlockSpec((1, tile, D_MODEL), row),
                  pl.BlockSpec((N_GM_GROUPS, CHUNK, CHUNK), lambda i, j: (0, 0, 0)),
                  pl.BlockSpec((CHUNK, D_GM), const2),
                  pl.BlockSpec((D_ATTN + D_GM, D_MODEL), const2),
                  pl.BlockSpec((1, 1, D_MODEL), per_b),
                  pl.BlockSpec((1, D_MODEL), const2),
                  pl.BlockSpec((1, 1, D_MODEL), per_b),
                  pl.BlockSpec((1, 1, D_MODEL), per_b),
                  pl.BlockSpec((N_EXPERTS, D_MODEL), const2)],
        out_specs=[pl.BlockSpec((1, tile, D_MODEL), row),
                   pl.BlockSpec((1, tile, D_MODEL), row),
                   pl.BlockSpec((1, N_EXPERTS, tile), lambda i, j: (i, 0, j))],
        out_shape=[jax.ShapeDtypeStruct((b, n, D_MODEL), F32),
                   jax.ShapeDtypeStruct((b, n, D_MODEL), BF16),
                   jax.ShapeDtypeStruct((b, N_EXPERTS, n), F32)],
        compiler_params=_params("arbitrary", "arbitrary"),
        name="mixout",
    )(o, u, vvn, x, w_s, bs, w_out, gt1, g_ffn, sh2, sc2, w_rt)


def _cumsum_lanes(m, tri):
    off = jnp.zeros((m.shape[0], 1), F32)
    outs = []
    for blk in range(m.shape[1] // 128):
        mb = m[:, blk * 128:(blk + 1) * 128]
        outs.append(_dot(mb.astype(BF16), tri) + off)
        off = off + jnp.sum(mb, axis=-1, keepdims=True)
    return jnp.concatenate(outs, axis=-1)


REFINE_STEPS = 32


def _route_kernel(cap, aff_ref, tri_ref, pos_ref):
    a = aff_ref[...]
    capf = float(cap)

    def count_ge(t):
        return jnp.sum(jnp.where(a >= t, 1.0, 0.0), axis=-1, keepdims=True)

    def bit_step(i, prefix):
        cand = prefix | jnp.left_shift(jnp.int32(1), 30 - i)
        return jnp.where(count_ge(pltpu.bitcast(cand, F32)) >= capf, cand, prefix)

    prefix = lax.fori_loop(0, 31, bit_step, jnp.zeros((a.shape[0], 1), jnp.int32))
    lo = pltpu.bitcast(prefix, F32)
    hi = pltpu.bitcast(prefix + 1, F32)

    def refine(_, carry):
        lo, hi = carry
        mid = 0.5 * lo + 0.5 * hi
        ok = count_ge(mid) >= capf
        return jnp.where(ok, mid, lo), jnp.where(ok, hi, mid)

    lo, hi = lax.fori_loop(0, REFINE_STEPS, refine, (lo, hi))
    thr = jnp.min(jnp.where(a >= lo, a, jnp.inf), axis=-1, keepdims=True)

    gt = jnp.where(a > thr, 1.0, 0.0)
    eq = jnp.where(a == thr, 1.0, 0.0)
    need = capf - jnp.sum(gt, axis=-1, keepdims=True)
    tri = tri_ref[...]
    sel = gt + eq * jnp.where(_cumsum_lanes(eq, tri) <= need, 1.0, 0.0)
    slot = _cumsum_lanes(sel, tri) - 1.0
    pos_ref[...] = jnp.where(sel > 0.0, slot, -1.0).astype(jnp.int32)


def _route(aff_t, tri, cap, rows):
    b, e, n = aff_t.shape
    pos = pl.pallas_call(
        functools.partial(_route_kernel, cap),
        grid=(b * e // rows,),
        in_specs=[pl.BlockSpec((rows, n), lambda i: (i, 0)),
                  pl.BlockSpec((128, 128), lambda i: (0, 0))],
        out_specs=pl.BlockSpec((rows, n), lambda i: (i, 0)),
        out_shape=jax.ShapeDtypeStruct((b * e, n), jnp.int32),
        compiler_params=_params("arbitrary"),
        name="route",
    )(aff_t.reshape(b * e, n), tri)
    return pos.reshape(b, e, n)


GATHER_EXPERTS = 8


def _gather_kernel(cap, pos_ref, aff_ref, h_ref, xe_ref, gate_ref, tok_ref):
    n = pos_ref.shape[-1]
    slot = lax.broadcasted_iota(jnp.int32, (cap, n), 0)
    tok = (lax.broadcasted_iota(jnp.int32, (cap, n), 1) * SLABS).astype(F32)
    for x in range(GATHER_EXPERTS):
        hit = pos_ref[0, x] == slot
        xe_ref[x] = _dot(jnp.where(hit, 1.0, 0.0).astype(BF16), h_ref[0]).astype(BF16)
        gate_ref[x] = jnp.sum(jnp.where(hit, aff_ref[0, x], 0.0), axis=-1, keepdims=True)
        tok_col = jnp.sum(jnp.where(hit, tok, 0.0), axis=-1, keepdims=True)
        tok_row = jnp.broadcast_to(tok_col, (cap, LANES)).T[0:1, :]
        tok_ref[0, x:x + 1, :] = tok_row.astype(jnp.int32)


def _gather(pos_t, aff_t, h2, cap):
    b, e, n = pos_t.shape
    row = lambda i, j: (i, j, 0, 0)
    slots = lambda i, j: (j, i, 0)
    return pl.pallas_call(
        functools.partial(_gather_kernel, cap),
        grid=(b, e // GATHER_EXPERTS),
        in_specs=[pl.BlockSpec((1, GATHER_EXPERTS, 1, n), row),
                  pl.BlockSpec((1, GATHER_EXPERTS, 1, n), row),
                  pl.BlockSpec((1, n, D_MODEL), lambda i, j: (i, 0, 0))],
        out_specs=[pl.BlockSpec((GATHER_EXPERTS, cap, D_MODEL), slots),
                   pl.BlockSpec((GATHER_EXPERTS, cap, 1), slots),
                   pl.BlockSpec((1, GATHER_EXPERTS, cap), lambda i, j: (i, j, 0))],
        out_shape=[jax.ShapeDtypeStruct((e, b * cap, D_MODEL), BF16),
                   jax.ShapeDtypeStruct((e, b * cap, 1), F32),
                   jax.ShapeDtypeStruct((b, e, cap), jnp.int32)],
        compiler_params=_params("arbitrary", "arbitrary"),
        name="gather",
    )(pos_t.reshape(b, e, 1, n), aff_t.reshape(b, e, 1, n), h2)


FF_CHUNK = 512


def _ffn_kernel(n_experts, xe_ref, gate_ref, w1_ref, w3_ref, w2_ref, y_ref, w1b, w3b, w2b):
    i = pl.program_id(0)
    j = pl.program_id(1)
    rows13 = w1_ref.shape[1]
    rows2 = w2_ref.shape[1]
    tile = xe_ref.shape[1]

    @pl.when(i < n_experts)
    def _():
        slot = i % 2
        w1b[slot, pl.ds(j * rows13, rows13), :] = w1_ref[0].astype(BF16)
        w3b[slot, pl.ds(j * rows13, rows13), :] = w3_ref[0].astype(BF16)
        w2b[slot, pl.ds(j * rows2, rows2), :] = w2_ref[0].astype(BF16)

    @pl.when(i == 0)
    def _():
        y_ref[...] = jnp.zeros_like(y_ref)

    @pl.when(i > 0)
    def _():
        slot = (i - 1) % 2
        xe = xe_ref[0]
        acc = jnp.zeros((tile, D_MODEL), F32)
        for c in range(EXPERT_FF // FF_CHUNK):
            cols = slice(c * FF_CHUNK, (c + 1) * FF_CHUNK)
            a = _dot(xe, w1b[slot, :, cols])
            g = _dot(xe, w3b[slot, :, cols])
            hid = (a / (1.0 + jnp.exp(-a))) * g
            acc = acc + _dot(hid.astype(BF16), w2b[slot, cols, :])
        y = acc * gate_ref[0]
        for s in range(SLABS):
            y_ref[0, pl.ds(s, tile, stride=SLABS), :] = y[:, s * LANES:(s + 1) * LANES]


def _ffn(xe, gate, w1, w3, w2, tile):
    e, m, _ = xe.shape
    steps = m // tile
    cur = lambda i, j: (jnp.maximum(i - 1, 0), j, 0)
    nxt = lambda i, j: (jnp.minimum(i, e - 1), j, 0)
    return pl.pallas_call(
        functools.partial(_ffn_kernel, e),
        grid=(e + 1, steps),
        in_specs=[pl.BlockSpec((1, tile, D_MODEL), cur),
                  pl.BlockSpec((1, tile, 1), cur),
                  pl.BlockSpec((1, D_MODEL // steps, EXPERT_FF), nxt),
                  pl.BlockSpec((1, D_MODEL // steps, EXPERT_FF), nxt),
                  pl.BlockSpec((1, EXPERT_FF // steps, D_MODEL), nxt)],
        out_specs=pl.BlockSpec((1, tile * SLABS, LANES), lambda i, j: (jnp.where(i == 0, e, i - 1), j, 0)),
        out_shape=jax.ShapeDtypeStruct((e + 1, m * SLABS, LANES), F32),
        scratch_shapes=[pltpu.VMEM((2, D_MODEL, EXPERT_FF), BF16),
                        pltpu.VMEM((2, D_MODEL, EXPERT_FF), BF16),
                        pltpu.VMEM((2, EXPERT_FF, D_MODEL), BF16)],
        compiler_params=_params("arbitrary", "arbitrary"),
        name="ffn",
    )(xe, gate, w1, w3, w2)


SCATTER_UNROLL = 8
SCATTER_EXPERTS = 4
FINAL_TILE = 512


def _combine_kernel(cap, n_scatter, tok_ref, y_ref, xn_ref, gt_ref, g_ref, o_ref, acc_ref):
    b = pl.program_id(0)
    k = pl.program_id(1)

    @pl.when(k == 0)
    def _():
        acc_ref[...] = jnp.zeros_like(acc_ref)

    def scatter_expert(x):
        base = (b * n_scatter * SCATTER_EXPERTS + k * SCATTER_EXPERTS + x) * cap
        for g in range(cap // SCATTER_UNROLL):
            rows, vals = [], []
            for u in range(SCATTER_UNROLL):
                r = g * SCATTER_UNROLL + u
                row = pl.ds(pl.multiple_of(tok_ref[base + r], SLABS), SLABS)
                rows.append(row)
                vals.append(acc_ref[row, :] + y_ref[x, r * SLABS:(r + 1) * SLABS, :])
            for row, val in zip(rows, vals):
                acc_ref[row, :] = val

    @pl.when(k < n_scatter)
    def _():
        for x in range(SCATTER_EXPERTS):
            scatter_expert(x)

    @pl.when(k >= n_scatter)
    def _():
        start = (k - n_scatter) * (FINAL_TILE * SLABS)
        moe = jnp.concatenate(
            [acc_ref[pl.ds(start + s, FINAL_TILE, stride=SLABS), :] for s in range(SLABS)],
            axis=-1)
        z = xn_ref[0] + gt_ref[0] * moe
        o_ref[0] = (z * _rms(z)) * g_ref[...]


def _combine(y, tok, xn, gt2, g_final, cap):
    b, n, _ = xn.shape
    e = y.shape[0] - 1
    n_scatter = e // SCATTER_EXPERTS
    rows = lambda i, k, tok: (i, jnp.maximum(k - n_scatter, 0), 0)
    return pl.pallas_call(
        functools.partial(_combine_kernel, cap, n_scatter),
        grid_spec=pltpu.PrefetchScalarGridSpec(
            num_scalar_prefetch=1,
            grid=(b, n_scatter + n // FINAL_TILE),
            in_specs=[pl.BlockSpec((SCATTER_EXPERTS, cap * SLABS, LANES),
                                   lambda i, k, tok: (jnp.minimum(k, n_scatter - 1), i, 0)),
                      pl.BlockSpec((1, FINAL_TILE, D_MODEL), rows),
                      pl.BlockSpec((1, 1, D_MODEL), lambda i, k, tok: (i, 0, 0)),
                      pl.BlockSpec((1, D_MODEL), lambda i, k, tok: (0, 0))],
            out_specs=pl.BlockSpec((1, FINAL_TILE, D_MODEL), rows),
            scratch_shapes=[pltpu.VMEM((n * SLABS, LANES), F32)]),
        out_shape=jax.ShapeDtypeStruct((b, n, D_MODEL), F32),
        compiler_params=_params("arbitrary", "arbitrary"),
        name="combine",
    )(tok, y, xn, gt2, g_final)


def _rope_tables(n):
    pos = jnp.arange(n)
    n_freq = HEAD_DIM // 4
    inv = ROPE_THETA ** (-jnp.arange(n_freq, dtype=F32) / n_freq)
    ang = jnp.concatenate([(pos // GRID_W)[:, None].astype(F32) * inv,
                           (pos % GRID_W)[:, None].astype(F32) * inv], axis=-1)
    cos, sin = jnp.cos(ang), jnp.sin(ang)
    cos_t = jnp.tile(jnp.concatenate([cos, cos], axis=-1), (1, N_Q_HEADS))
    sin_t = jnp.tile(jnp.concatenate([-sin, sin], axis=-1), (1, N_Q_HEADS))
    return cos_t, sin_t


def _block_diag_ones(width, block):
    i = jnp.arange(width) // block
    return (i[:, None] == i[None, :]).astype(BF16)


def kernel(x, c, ctx, c_ctx, w_mod, b_mod, g_mix, g_ffn, w_in, q_gain, k_gain, v_gain, w_s, b_s,
           w_out, w_router, w1, w3, w2, g_final):
    assert w_mod.shape[0] == 1, "single-layer kernel"
    b, n, d = x.shape
    cap = CAPACITY_FACTOR * n // N_EXPERTS

    rows = -(-(b + 1) // 8) * 8
    cvecs = jnp.concatenate([c, c_ctx[None, :], jnp.zeros((rows - b - 1, d), F32)], axis=0)
    mods = _adaln(cvecs, w_mod[0], b_mod[0])
    sh1, sc1, gt1, sh2, sc2, gt2 = [m[:b, None, :] for m in jnp.split(mods, 6, axis=-1)]
    csh1, csc1 = mods[b:b + 1, :d], mods[b:b + 1, d:2 * d]

    cos_t, sin_t = _rope_tables(n)
    bd = _block_diag_ones(256, HEAD_DIM)
    w_in_b = w_in[0].astype(BF16)
    g_mix2 = g_mix[0][None, :]
    qg = jnp.tile(q_gain[0], N_Q_HEADS)[None, :]
    kg = jnp.tile(k_gain[0], N_KV_HEADS)[None, :]

    q, kl, vlt, u, vvn = _inproj(x, sh1, sc1, g_mix2, w_in_b, bd, qg, kg, v_gain[0][None, :],
                                cos_t, sin_t, tile=512)
    kc, vct = _ctxkv(ctx, csh1, csc1, g_mix2, w_in_b, bd, kg)
    score_bound = (BOUND_MARGIN * HEAD_DIM * Q_SCALE) * jnp.max(jnp.abs(q_gain[0])) * jnp.max(jnp.abs(k_gain[0]))
    o = _attn(q, kl, kc, vlt, vct, score_bound, tile=512)

    bs = jnp.repeat(b_s[0].T, GM_GROUP_DIM, axis=1)
    xn, h2, aff_t = _mixout(o, u, vvn, x, w_s[0].astype(BF16), bs, w_out[0].astype(BF16), gt1,
                            g_ffn[0][None, :], sh2, sc2, w_router[0].T.astype(BF16), tile=512)

    tri = (jnp.arange(128)[:, None] <= jnp.arange(128)[None, :]).astype(BF16)
    pos_t = _route(aff_t, tri, cap, rows=64)
    xe, gate, tok = _gather(pos_t, aff_t, h2, cap)
    y = _ffn(xe, gate, w1[0], w3[0], w2[0], tile=512)
    return _combine(y, tok.reshape(-1), xn, gt2, g_final[None, :], cap)
```

```python
import functools
import math

import jax
import jax.numpy as jnp
from jax import lax
from jax.experimental import pallas as pl
from jax.experimental.pallas import tpu as pltpu

D_MODEL = 1024
CTX_LEN = 256
GRID_W = 64
HEAD_DIM = 64
N_Q_HEADS = 8
N_KV_HEADS = 2
Q_PER_KV = N_Q_HEADS // N_KV_HEADS
D_ATTN = N_Q_HEADS * HEAD_DIM
D_KV = N_KV_HEADS * HEAD_DIM
N_GM_GROUPS = 8
GM_GROUP_DIM = 64
D_GM = N_GM_GROUPS * GM_GROUP_DIM
CHUNK = 128
D_IN = D_ATTN + 2 * D_KV + 2 * D_GM
ROPE_THETA = 10000.0
N_EXPERTS = 16
CAPACITY_FACTOR = 2
EXPERT_FF = 2048
EPS = 1e-6

F32 = jnp.float32
BF16 = jnp.bfloat16

VMEM_LIMIT_BYTES = 52 * 1024 * 1024
SUB_ROWS = 256
LANES = 128
SLABS = D_MODEL // LANES

K_OFF = D_ATTN
V_OFF = D_ATTN + D_KV
GM_OFF = D_ATTN + 2 * D_KV

Q_SCALE = math.log2(math.e) * HEAD_DIM ** -0.5


def _params(*sem):
    return pltpu.CompilerParams(dimension_semantics=sem, vmem_limit_bytes=VMEM_LIMIT_BYTES)


def _dot(a, b):
    return jnp.dot(a, b, preferred_element_type=F32)


def _dot_nt(a, b):
    return lax.dot_general(a, b, (((1,), (1,)), ((), ())), preferred_element_type=F32)


def _rms(v):
    return lax.rsqrt(jnp.sum(v * v, axis=-1, keepdims=True) * (1.0 / v.shape[-1]) + EPS)


def _modulate(x, g, shift, scale):
    return ((x * _rms(x)) * g) * (1.0 + scale) + shift


def _seg_sum(v, bd):
    w = bd.shape[0]
    outs = []
    for j in range(v.shape[1] // w):
        c = v[:, j * w:(j + 1) * w]
        hi = c.astype(BF16)
        lo = (c - hi.astype(F32)).astype(BF16)
        outs.append(_dot(hi, bd) + _dot(lo, bd))
    return outs[0] if len(outs) == 1 else jnp.concatenate(outs, axis=-1)


def _head_norm(z, bd, gain):
    ss = _seg_sum(z * z, bd)
    return (z * lax.rsqrt(ss * (1.0 / HEAD_DIM) + EPS)) * gain


def _rope(v, cos, sin_signed):
    w = v.shape[-1]
    lane = lax.broadcasted_iota(jnp.int32, v.shape, 1)
    partner = jnp.where((lane & (HEAD_DIM // 2)) == 0,
                        pltpu.roll(v, w - HEAD_DIM // 2, axis=1),
                        pltpu.roll(v, HEAD_DIM // 2, axis=1))
    return v * cos + partner * sin_signed


def _gelu(z):
    return 0.5 * z * (1.0 + lax.erf(z * math.sqrt(0.5)))


def _adaln_kernel(c_ref, w_ref, b_ref, o_ref):
    c = c_ref[...]
    s = c / (1.0 + jnp.exp(-c))
    o_ref[...] = _dot(s.astype(BF16), w_ref[...].astype(BF16)) + b_ref[...]


def _adaln(cvecs, w_mod, b_mod):
    rows = cvecs.shape[0]
    n_out = w_mod.shape[1]
    tn = 768
    return pl.pallas_call(
        _adaln_kernel,
        grid=(n_out // tn,),
        in_specs=[pl.BlockSpec((rows, D_MODEL), lambda j: (0, 0)),
                  pl.BlockSpec((D_MODEL, tn), lambda j: (0, j)),
                  pl.BlockSpec((1, tn), lambda j: (0, j))],
        out_specs=pl.BlockSpec((rows, tn), lambda j: (0, j)),
        out_shape=jax.ShapeDtypeStruct((rows, n_out), F32),
        compiler_params=_params("arbitrary"),
        name="adaln",
    )(cvecs, w_mod, b_mod.reshape(1, n_out))


def _inproj_kernel(x_ref, sh_ref, sc_ref, g_ref, w_ref, bd_ref, qg_ref, kg_ref, vg_ref,
                   cos_ref, sin_ref, q_ref, k_ref, vt_ref, u_ref, vv_ref):
    bd = bd_ref[...]
    for r0 in range(0, x_ref.shape[1], SUB_ROWS):
        rows = slice(r0, r0 + SUB_ROWS)
        h = _modulate(x_ref[0, rows, :], g_ref[...], sh_ref[0], sc_ref[0])
        z = _dot(h.astype(BF16), w_ref[...])
        cos = cos_ref[rows, :]
        sin = sin_ref[rows, :]

        q = _rope(_head_norm(z[:, :D_ATTN], bd, qg_ref[...]), cos, sin)
        q_ref[0, rows, :] = (q * Q_SCALE).astype(BF16)

        k = _head_norm(z[:, K_OFF:K_OFF + D_KV], bd[:D_KV, :D_KV], kg_ref[...])
        k_ref[0, rows, :] = _rope(k, cos[:, :D_KV], sin[:, :D_KV]).astype(BF16)
        vt_ref[0, :, rows] = z[:, V_OFF:V_OFF + D_KV].T.astype(BF16)

        gz = _gelu(z[:, GM_OFF:])
        u_ref[0, rows, :] = gz[:, :D_GM]
        vv = gz[:, D_GM:]
        vv_ref[0, rows, :] = ((vv * _rms(vv)) * vg_ref[...]).astype(BF16)


def _inproj(x, sh1, sc1, g_mix, w_in, bd, qg, kg, vg, cos_t, sin_t, tile):
    b, n, _ = x.shape
    row = lambda i, j: (j, i, 0)
    per_b = lambda i, j: (j, 0, 0)
    const = lambda i, j: (0, 0)
    tab = lambda i, j: (i, 0)
    return pl.pallas_call(
        _inproj_kernel,
        grid=(n // tile, b),
        in_specs=[pl.BlockSpec((1, tile, D_MODEL), row),
                  pl.BlockSpec((1, 1, D_MODEL), per_b),
                  pl.BlockSpec((1, 1, D_MODEL), per_b),
                  pl.BlockSpec((1, D_MODEL), const),
                  pl.BlockSpec((D_MODEL, D_IN), const),
                  pl.BlockSpec(bd.shape, const),
                  pl.BlockSpec((1, D_ATTN), const),
                  pl.BlockSpec((1, D_KV), const),
                  pl.BlockSpec((1, D_GM), const),
                  pl.BlockSpec((tile, D_ATTN), tab),
                  pl.BlockSpec((tile, D_ATTN), tab)],
        out_specs=[pl.BlockSpec((1, tile, D_ATTN), row),
                   pl.BlockSpec((1, tile, D_KV), row),
                   pl.BlockSpec((1, D_KV, tile), lambda i, j: (j, 0, i)),
                   pl.BlockSpec((1, tile, D_GM), row),
                   pl.BlockSpec((1, tile, D_GM), row)],
        out_shape=[jax.ShapeDtypeStruct((b, n, D_ATTN), BF16),
                   jax.ShapeDtypeStruct((b, n, D_KV), BF16),
                   jax.ShapeDtypeStruct((b, D_KV, n), BF16),
                   jax.ShapeDtypeStruct((b, n, D_GM), F32),
                   jax.ShapeDtypeStruct((b, n, D_GM), BF16)],
        compiler_params=_params("arbitrary", "arbitrary"),
        name="inproj",
    )(x, sh1, sc1, g_mix, w_in, bd, qg, kg, vg, cos_t, sin_t)


def _ctxkv_kernel(x_ref, sh_ref, sc_ref, g_ref, w_ref, bd_ref, kg_ref, k_ref, vt_ref):
    h = _modulate(x_ref[0], g_ref[...], sh_ref[...], sc_ref[...])
    z = _dot(h.astype(BF16), w_ref[:, K_OFF:K_OFF + 2 * D_KV])
    k_ref[0] = _head_norm(z[:, :D_KV], bd_ref[:D_KV, :D_KV], kg_ref[...]).astype(BF16)
    vt_ref[0] = z[:, D_KV:].T.astype(BF16)


def _ctxkv(ctx, csh1, csc1, g_mix, w_in, bd, kg):
    b, n, _ = ctx.shape
    const = lambda i: (0, 0)
    row = lambda i: (i, 0, 0)
    return pl.pallas_call(
        _ctxkv_kernel,
        grid=(b,),
        in_specs=[pl.BlockSpec((1, n, D_MODEL), row),
                  pl.BlockSpec((1, D_MODEL), const),
                  pl.BlockSpec((1, D_MODEL), const),
                  pl.BlockSpec((1, D_MODEL), const),
                  pl.BlockSpec((D_MODEL, D_IN), const),
                  pl.BlockSpec(bd.shape, const),
                  pl.BlockSpec((1, D_KV), const)],
        out_specs=[pl.BlockSpec((1, n, D_KV), row),
                   pl.BlockSpec((1, D_KV, n), row)],
        out_shape=[jax.ShapeDtypeStruct((b, n, D_KV), BF16),
                   jax.ShapeDtypeStruct((b, D_KV, n), BF16)],
        compiler_params=_params("arbitrary"),
        name="ctxkv",
    )(ctx, csh1, csc1, g_mix, w_in, bd, kg)


ATTN_LOOKAHEAD = 4
BOUND_MARGIN = 1.02
MAX_SAFE_SCORE_BOUND = 60.0


def _head_slices(q_ref, kl_ref, kc_ref, h):
    kv = slice((h // Q_PER_KV) * HEAD_DIM, (h // Q_PER_KV + 1) * HEAD_DIM)
    qh = q_ref[0, :, h * HEAD_DIM:(h + 1) * HEAD_DIM]
    return qh, kl_ref[0, :, kv], kc_ref[0, :, kv]


def _weighted_values(vlt_ref, vct_ref, p_ref, h, buf, n_lat):
    kv = slice((h // Q_PER_KV) * HEAD_DIM, (h // Q_PER_KV + 1) * HEAD_DIM)
    return (_dot(vlt_ref[0, kv, :], p_ref[buf, :n_lat, :])
            + _dot(vct_ref[0, kv, :], p_ref[buf, n_lat:, :]))


def _attn_kernel(q_ref, kl_ref, kc_ref, vlt_ref, vct_ref, o_ref, st_ref, pt_ref):
    n_lat = kl_ref.shape[1]

    def scores(h):
        qh, kl, kc = _head_slices(q_ref, kl_ref, kc_ref, h)
        st_ref[h, :n_lat, :] = _dot_nt(kl, qh)
        st_ref[h, n_lat:, :] = _dot_nt(kc, qh)

    for h in range(ATTN_LOOKAHEAD):
        scores(h)
    outs = []
    for h in range(N_Q_HEADS):
        if h + ATTN_LOOKAHEAD < N_Q_HEADS:
            scores(h + ATTN_LOOKAHEAD)
        st = st_ref[h]
        pt = jnp.exp2(st - jnp.max(st, axis=0, keepdims=True))
        l = jnp.sum(pt, axis=0, keepdims=True)
        pt_ref[h % 2] = pt.astype(BF16)
        outs.append(_weighted_values(vlt_ref, vct_ref, pt_ref, h, h % 2, n_lat) / l)
    o_ref[0] = jnp.concatenate(outs, axis=0).T.astype(BF16)


def _attn_bounded_kernel(shift_ref, q_ref, kl_ref, kc_ref, vlt_ref, vct_ref, o_ref, pt_ref):
    n_lat = kl_ref.shape[1]
    shift = shift_ref[...]
    sums = [None] * N_Q_HEADS

    def probs(h):
        qh, kl, kc = _head_slices(q_ref, kl_ref, kc_ref, h)
        pl_t = jnp.exp2(_dot_nt(kl, qh) - shift)
        pc_t = jnp.exp2(_dot_nt(kc, qh) - shift)
        sums[h] = jnp.sum(pl_t, axis=0, keepdims=True) + jnp.sum(pc_t, axis=0, keepdims=True)
        pt_ref[h, :n_lat, :] = pl_t.astype(BF16)
        pt_ref[h, n_lat:, :] = pc_t.astype(BF16)

    for h in range(ATTN_LOOKAHEAD):
        probs(h)
    outs = []
    for h in range(N_Q_HEADS):
        if h + ATTN_LOOKAHEAD < N_Q_HEADS:
            probs(h + ATTN_LOOKAHEAD)
        outs.append(_weighted_values(vlt_ref, vct_ref, pt_ref, h, h, n_lat) / sums[h])
    o_ref[0] = jnp.concatenate(outs, axis=0).T.astype(BF16)


def _attn(q, kl, kc, vlt, vct, score_bound, tile):
    b, n, _ = q.shape
    n_ctx = kc.shape[1]
    nk = n + n_ctx
    specs = dict(
        grid=(b, n // tile),
        out_specs=pl.BlockSpec((1, tile, D_ATTN), lambda i, j: (i, j, 0)),
        out_shape=jax.ShapeDtypeStruct((b, n, D_ATTN), BF16),
        compiler_params=_params("arbitrary", "arbitrary"))
    per_b = lambda i, j: (i, 0, 0)
    qkv_specs = [pl.BlockSpec((1, tile, D_ATTN), lambda i, j: (i, j, 0)),
                 pl.BlockSpec((1, n, D_KV), per_b),
                 pl.BlockSpec((1, n_ctx, D_KV), per_b),
                 pl.BlockSpec((1, D_KV, n), per_b),
                 pl.BlockSpec((1, D_KV, n_ctx), per_b)]

    def exact_max(score_bound, *qkv):
        return pl.pallas_call(
            _attn_kernel, in_specs=qkv_specs,
            scratch_shapes=[pltpu.VMEM((N_Q_HEADS, nk, tile), F32), pltpu.VMEM((2, nk, tile), BF16)],
            name="attn", **specs,
        )(*qkv)

    def bounded(score_bound, *qkv):
        return pl.pallas_call(
            _attn_bounded_kernel, in_specs=[pl.BlockSpec((1, tile), lambda i, j: (0, 0))] + qkv_specs,
            scratch_shapes=[pltpu.VMEM((N_Q_HEADS, nk, tile), BF16)], name="attn_bounded", **specs,
        )(jnp.full((1, tile), score_bound, F32), *qkv)

    return lax.cond(score_bound <= MAX_SAFE_SCORE_BOUND, bounded, exact_max,
                    score_bound, q, kl, kc, vlt, vct)


def _mixout_kernel(o_ref, u_ref, vv_ref, x_ref, ws_ref, bs_ref, wout_ref, gt_ref, g_ref,
                   sh_ref, sc_ref, wr_ref, xn_ref, h_ref, aff_ref):
    tile = o_ref.shape[1]
    bs = bs_ref[...]
    gms = []
    for c in range(tile // CHUNK):
        vc = vv_ref[0, c * CHUNK:(c + 1) * CHUNK, :]
        mixed = jnp.concatenate(
            [_dot(ws_ref[g], vc[:, g * GM_GROUP_DIM:(g + 1) * GM_GROUP_DIM]) for g in range(N_GM_GROUPS)],
            axis=-1)
        gms.append(u_ref[0, c * CHUNK:(c + 1) * CHUNK, :] * (mixed + bs))
    gm = jnp.concatenate(gms, axis=0).astype(BF16)

    proj = _dot(o_ref[0], wout_ref[:D_ATTN, :]) + _dot(gm, wout_ref[D_ATTN:, :])
    xn = x_ref[0] + gt_ref[0] * proj
    xn_ref[0] = xn

    h = _modulate(xn, g_ref[...], sh_ref[0], sc_ref[0]).astype(BF16)
    h_ref[0] = h
    logits = _dot_nt(wr_ref[...], h)
    e = jnp.exp(logits - jnp.max(logits, axis=0, keepdims=True))
    aff_ref[0] = e / jnp.sum(e, axis=0, keepdims=True)


def _mixout(o, u, vvn, x, w_s, bs, w_out, gt1, g_ffn, sh2, sc2, w_rt, tile):
    b, n, _ = x.shape
    row = lambda i, j: (i, j, 0)
    per_b = lambda i, j: (i, 0, 0)
    const2 = lambda i, j: (0, 0)
    return pl.pallas_call(
        _mixout_kernel,
        grid=(b, n // tile),
        in_specs=[pl.BlockSpec((1, tile, D_ATTN), row),
                  pl.BlockSpec((1, tile, D_GM), row),
                  pl.BlockSpec((1, tile, D_GM), row),
                  pl.BlockSpec((1, tile, D_MODEL), row),
                  pl.BlockSpec((N_GM_GROUPS, CHUNK, CHUNK), lambda i, j: (0, 0, 0)),
                  pl.BlockSpec((CHUNK, D_GM), const2),
                  pl.BlockSpec((D_ATTN + D_GM, D_MODEL), const2),
                  pl.BlockSpec((1, 1, D_MODEL), per_b),
                  pl.BlockSpec((1, D_MODEL), const2),
                  pl.BlockSpec((1, 1, D_MODEL), per_b),
                  pl.BlockSpec((1, 1, D_MODEL), per_b),
                  pl.BlockSpec((N_EXPERTS, D_MODEL), const2)],
        out_specs=[pl.BlockSpec((1, tile, D_MODEL), row),
                   pl.BlockSpec((1, tile, D_MODEL), row),
                   pl.BlockSpec((1, N_EXPERTS, tile), lambda i, j: (i, 0, j))],
        out_shape=[jax.ShapeDtypeStruct((b, n, D_MODEL), F32),
                   jax.ShapeDtypeStruct((b, n, D_MODEL), BF16),
                   jax.ShapeDtypeStruct((b, N_EXPERTS, n), F32)],
        compiler_params=_params("arbitrary", "arbitrary"),
        name="mixout",
    )(o, u, vvn, x, w_s, bs, w_out, gt1, g_ffn, sh2, sc2, w_rt)


def _cumsum_lanes(m, tri):
    off = jnp.zeros((m.shape[0], 1), F32)
    outs = []
    for blk in range(m.shape[1] // 128):
        mb = m[:, blk * 128:(blk + 1) * 128]
        outs.append(_dot(mb.astype(BF16), tri) + off)
        off = off + jnp.sum(mb, axis=-1, keepdims=True)
    return jnp.concatenate(outs, axis=-1)


REFINE_STEPS = 32


def _route_kernel(cap, aff_ref, tri_ref, pos_ref):
    a = aff_ref[...]
    capf = float(cap)

    def count_ge(t):
        return jnp.sum(jnp.where(a >= t, 1.0, 0.0), axis=-1, keepdims=True)

    def bit_step(i, prefix):
        cand = prefix | jnp.left_shift(jnp.int32(1), 30 - i)
        return jnp.where(count_ge(pltpu.bitcast(cand, F32)) >= capf, cand, prefix)

    prefix = lax.fori_loop(0, 31, bit_step, jnp.zeros((a.shape[0], 1), jnp.int32))
    lo = pltpu.bitcast(prefix, F32)
    hi = pltpu.bitcast(prefix + 1, F32)

    def refine(_, carry):
        lo, hi = carry
        mid = 0.5 * lo + 0.5 * hi
        ok = count_ge(mid) >= capf
        return jnp.where(ok, mid, lo), jnp.where(ok, hi, mid)

    lo, hi = lax.fori_loop(0, REFINE_STEPS, refine, (lo, hi))
    thr = jnp.min(jnp.where(a >= lo, a, jnp.inf), axis=-1, keepdims=True)

    gt = jnp.where(a > thr, 1.0, 0.0)
    eq = jnp.where(a == thr, 1.0, 0.0)
    need = capf - jnp.sum(gt, axis=-1, keepdims=True)
    tri = tri_ref[...]
    sel = gt + eq * jnp.where(_cumsum_lanes(eq, tri) <= need, 1.0, 0.0)
    slot = _cumsum_lanes(sel, tri) - 1.0
    pos_ref[...] = jnp.where(sel > 0.0, slot, -1.0).astype(jnp.int32)


def _route(aff_t, tri, cap, rows):
    b, e, n = aff_t.shape
    pos = pl.pallas_call(
        functools.partial(_route_kernel, cap),
        grid=(b * e // rows,),
        in_specs=[pl.BlockSpec((rows, n), lambda i: (i, 0)),
                  pl.BlockSpec((128, 128), lambda i: (0, 0))],
        out_specs=pl.BlockSpec((rows, n), lambda i: (i, 0)),
        out_shape=jax.ShapeDtypeStruct((b * e, n), jnp.int32),
        compiler_params=_params("arbitrary"),
        name="route",
    )(aff_t.reshape(b * e, n), tri)
    return pos.reshape(b, e, n)


GATHER_EXPERTS = 8


def _gather_kernel(cap, pos_ref, aff_ref, h_ref, xe_ref, gate_ref, tok_ref):
    n = pos_ref.shape[-1]
    slot = lax.broadcasted_iota(jnp.int32, (cap, n), 0)
    tok = (lax.broadcasted_iota(jnp.int32, (cap, n), 1) * SLABS).astype(F32)
    for x in range(GATHER_EXPERTS):
        hit = pos_ref[0, x:x + 1, :] == slot
        xe_ref[x] = _dot(jnp.where(hit, 1.0, 0.0).astype(BF16), h_ref[0]).astype(BF16)
        gate_ref[x] = jnp.sum(jnp.where(hit, aff_ref[0, x:x + 1, :], 0.0), axis=-1, keepdims=True)
        tok_col = jnp.sum(jnp.where(hit, tok, 0.0), axis=-1, keepdims=True)
        tok_row = jnp.broadcast_to(tok_col, (cap, LANES)).T[0:1, :]
        tok_ref[0, x:x + 1, :] = tok_row.astype(jnp.int32)


def _gather(pos_t, aff_t, h2, cap):
    b, e, n = pos_t.shape
    row = lambda i, j: (i, j, 0)
    slots = lambda i, j: (j, i, 0)
    return pl.pallas_call(
        functools.partial(_gather_kernel, cap),
        grid=(b, e // GATHER_EXPERTS),
        in_specs=[pl.BlockSpec((1, GATHER_EXPERTS, n), row),
                  pl.BlockSpec((1, GATHER_EXPERTS, n), row),
                  pl.BlockSpec((1, n, D_MODEL), lambda i, j: (i, 0, 0))],
        out_specs=[pl.BlockSpec((GATHER_EXPERTS, cap, D_MODEL), slots),
                   pl.BlockSpec((GATHER_EXPERTS, cap, 1), slots),
                   pl.BlockSpec((1, GATHER_EXPERTS, cap), lambda i, j: (i, j, 0))],
        out_shape=[jax.ShapeDtypeStruct((e, b * cap, D_MODEL), BF16),
                   jax.ShapeDtypeStruct((e, b * cap, 1), F32),
                   jax.ShapeDtypeStruct((b, e, cap), jnp.int32)],
        compiler_params=_params("arbitrary", "arbitrary"),
        name="gather",
    )(pos_t, aff_t, h2)


FF_CHUNK = 512


def _ffn_kernel(xe_ref, gate_ref, w1_ref, w3_ref, w2_ref, y_ref, w1a, w3a, w2a, w1b, w3b, w2b):
    i = pl.program_id(0)
    j = pl.program_id(1)
    rows13 = w1_ref.shape[1]
    rows2 = w2_ref.shape[1]
    tile = xe_ref.shape[1]
    sets = ((w1a, w3a, w2a), (w1b, w3b, w2b))

    def stage(dst):
        w1s, w3s, w2s = dst
        w1s[pl.ds(j * rows13, rows13), :] = w1_ref[0].astype(BF16)
        w3s[pl.ds(j * rows13, rows13), :] = w3_ref[0].astype(BF16)
        w2s[pl.ds(j * rows2, rows2), :] = w2_ref[0].astype(BF16)

    def compute(src):
        w1s, w3s, w2s = src
        xe = xe_ref[0]
        acc = jnp.zeros((tile, D_MODEL), F32)
        for c in range(EXPERT_FF // FF_CHUNK):
            cols = slice(c * FF_CHUNK, (c + 1) * FF_CHUNK)
            a = _dot(xe, w1s[:, cols])
            g = _dot(xe, w3s[:, cols])
            hid = (a / (1.0 + jnp.exp(-a))) * g
            acc = acc + _dot(hid.astype(BF16), w2s[cols, :])
        y = acc * gate_ref[0]
        for s in range(SLABS):
            y_ref[0, pl.ds(s, tile, stride=SLABS), :] = y[:, s * LANES:(s + 1) * LANES]

    @pl.when(i == 0)
    def _():
        stage(sets[0])
        y_ref[...] = jnp.zeros_like(y_ref)

    for parity in (0, 1):
        @pl.when((i > 0) & (i % 2 == parity))
        def _():
            compute(sets[1 - parity])
            stage(sets[parity])


def _ffn(xe, gate, w1, w3, w2, tile):
    e, m, _ = xe.shape
    steps = m // tile
    cur = lambda i, j: (jnp.maximum(i - 1, 0), j, 0)
    nxt = lambda i, j: (jnp.minimum(i, e - 1), j, 0)
    return pl.pallas_call(
        _ffn_kernel,
        grid=(e + 1, steps),
        in_specs=[pl.BlockSpec((1, tile, D_MODEL), cur),
                  pl.BlockSpec((1, tile, 1), cur),
                  pl.BlockSpec((1, D_MODEL // steps, EXPERT_FF), nxt),
                  pl.BlockSpec((1, D_MODEL // steps, EXPERT_FF), nxt),
                  pl.BlockSpec((1, EXPERT_FF // steps, D_MODEL), nxt)],
        out_specs=pl.BlockSpec((1, tile * SLABS, LANES), lambda i, j: (jnp.where(i == 0, e, i - 1), j, 0)),
        out_shape=jax.ShapeDtypeStruct((e + 1, m * SLABS, LANES), F32),
        scratch_shapes=[pltpu.VMEM((D_MODEL, EXPERT_FF), BF16),
                        pltpu.VMEM((D_MODEL, EXPERT_FF), BF16),
                        pltpu.VMEM((EXPERT_FF, D_MODEL), BF16)] * 2,
        compiler_params=_params("arbitrary", "arbitrary"),
        name="ffn",
    )(xe, gate, w1, w3, w2)


SCATTER_UNROLL = 8
SCATTER_EXPERTS = 4
FINAL_TILE = 512


def _combine_kernel(cap, n_scatter, tok_ref, y_ref, xn_ref, gt_ref, g_ref, o_ref, acc_ref):
    b = pl.program_id(0)
    k = pl.program_id(1)

    @pl.when(k == 0)
    def _():
        acc_ref[...] = jnp.zeros_like(acc_ref)

    def scatter_expert(x):
        base = (b * n_scatter * SCATTER_EXPERTS + k * SCATTER_EXPERTS + x) * cap
        for g in range(cap // SCATTER_UNROLL):
            rows, vals = [], []
            for u in range(SCATTER_UNROLL):
                r = g * SCATTER_UNROLL + u
                row = pl.ds(pl.multiple_of(tok_ref[base + r], SLABS), SLABS)
                rows.append(row)
                vals.append(acc_ref[row, :] + y_ref[x, r * SLABS:(r + 1) * SLABS, :])
            for row, val in zip(rows, vals):
                acc_ref[row, :] = val

    @pl.when(k < n_scatter)
    def _():
        for x in range(SCATTER_EXPERTS):
            scatter_expert(x)

    @pl.when(k >= n_scatter)
    def _():
        start = (k - n_scatter) * (FINAL_TILE * SLABS)
        moe = jnp.concatenate(
            [acc_ref[pl.ds(start + s, FINAL_TILE, stride=SLABS), :] for s in range(SLABS)],
            axis=-1)
        z = xn_ref[0] + gt_ref[0] * moe
        o_ref[0] = (z * _rms(z)) * g_ref[...]


def _combine(y, tok, xn, gt2, g_final, cap):
    b, n, _ = xn.shape
    e = y.shape[0] - 1
    n_scatter = e // SCATTER_EXPERTS
    rows = lambda i, k, tok: (i, jnp.maximum(k - n_scatter, 0), 0)
    return pl.pallas_call(
        functools.partial(_combine_kernel, cap, n_scatter),
        grid_spec=pltpu.PrefetchScalarGridSpec(
            num_scalar_prefetch=1,
            grid=(b, n_scatter + n // FINAL_TILE),
            in_specs=[pl.BlockSpec((SCATTER_EXPERTS, cap * SLABS, LANES),
                                   lambda i, k, tok: (jnp.minimum(k, n_scatter - 1), i, 0)),
                      pl.BlockSpec((1, FINAL_TILE, D_MODEL), rows),
                      pl.BlockSpec((1, 1, D_MODEL), lambda i, k, tok: (i, 0, 0)),
                      pl.BlockSpec((1, D_MODEL), lambda i, k, tok: (0, 0))],
            out_specs=pl.BlockSpec((1, FINAL_TILE, D_MODEL), rows),
            scratch_shapes=[pltpu.VMEM((n * SLABS, LANES), F32)]),
        out_shape=jax.ShapeDtypeStruct((b, n, D_MODEL), F32),
        compiler_params=_params("arbitrary", "arbitrary"),
        name="combine",
    )(tok, y, xn, gt2, g_final)


def _rope_tables(n):
    pos = jnp.arange(n)
    n_freq = HEAD_DIM // 4
    inv = ROPE_THETA ** (-jnp.arange(n_freq, dtype=F32) / n_freq)
    ang = jnp.concatenate([(pos // GRID_W)[:, None].astype(F32) * inv,
                           (pos % GRID_W)[:, None].astype(F32) * inv], axis=-1)
    cos, sin = jnp.cos(ang), jnp.sin(ang)
    cos_t = jnp.tile(jnp.concatenate([cos, cos], axis=-1), (1, N_Q_HEADS))
    sin_t = jnp.tile(jnp.concatenate([-sin, sin], axis=-1), (1, N_Q_HEADS))
    return cos_t, sin_t


def _block_diag_ones(width, block):
    i = jnp.arange(width) // block
    return (i[:, None] == i[None, :]).astype(BF16)


def kernel(x, c, ctx, c_ctx, w_mod, b_mod, g_mix, g_ffn, w_in, q_gain, k_gain, v_gain, w_s, b_s,
           w_out, w_router, w1, w3, w2, g_final):
    assert w_mod.shape[0] == 1, "single-layer kernel"
    b, n, d = x.shape
    cap = CAPACITY_FACTOR * n // N_EXPERTS

    rows = -(-(b + 1) // 8) * 8
    cvecs = jnp.concatenate([c, c_ctx[None, :], jnp.zeros((rows - b - 1, d), F32)], axis=0)
    mods = _adaln(cvecs, w_mod[0], b_mod[0])
    sh1, sc1, gt1, sh2, sc2, gt2 = [m[:b, None, :] for m in jnp.split(mods, 6, axis=-1)]
    csh1, csc1 = mods[b:b + 1, :d], mods[b:b + 1, d:2 * d]

    cos_t, sin_t = _rope_tables(n)
    bd = _block_diag_ones(256, HEAD_DIM)
    w_in_b = w_in[0].astype(BF16)
    g_mix2 = g_mix[0][None, :]
    qg = jnp.tile(q_gain[0], N_Q_HEADS)[None, :]
    kg = jnp.tile(k_gain[0], N_KV_HEADS)[None, :]

    q, kl, vlt, u, vvn = _inproj(x, sh1, sc1, g_mix2, w_in_b, bd, qg, kg, v_gain[0][None, :],
                                cos_t, sin_t, tile=512)
    kc, vct = _ctxkv(ctx, csh1, csc1, g_mix2, w_in_b, bd, kg)
    score_bound = (BOUND_MARGIN * HEAD_DIM * Q_SCALE) * jnp.max(jnp.abs(q_gain[0])) * jnp.max(jnp.abs(k_gain[0]))
    o = _attn(q, kl, kc, vlt, vct, score_bound, tile=512)

    bs = jnp.repeat(b_s[0].T, GM_GROUP_DIM, axis=1)
    xn, h2, aff_t = _mixout(o, u, vvn, x, w_s[0].astype(BF16), bs, w_out[0].astype(BF16), gt1,
                            g_ffn[0][None, :], sh2, sc2, w_router[0].T.astype(BF16), tile=512)

    tri = (jnp.arange(128)[:, None] <= jnp.arange(128)[None, :]).astype(BF16)
    pos_t = _route(aff_t, tri, cap, rows=64)
    xe, gate, tok = _gather(pos_t, aff_t, h2, cap)
    y = _ffn(xe, gate, w1[0], w3[0], w2[0], tile=512)
    return _combine(y, tok.reshape(-1), xn, gt2, g_final[None, :], cap)
```

```python
import functools
import math

import jax
import jax.numpy as jnp
from jax import lax
from jax.experimental import pallas as pl
from jax.experimental.pallas import tpu as pltpu

D_MODEL = 1024
CTX_LEN = 256
GRID_W = 64
HEAD_DIM = 64
N_Q_HEADS = 8
N_KV_HEADS = 2
Q_PER_KV = N_Q_HEADS // N_KV_HEADS
D_ATTN = N_Q_HEADS * HEAD_DIM
D_KV = N_KV_HEADS * HEAD_DIM
N_GM_GROUPS = 8
GM_GROUP_DIM = 64
D_GM = N_GM_GROUPS * GM_GROUP_DIM
CHUNK = 128
D_IN = D_ATTN + 2 * D_KV + 2 * D_GM
ROPE_THETA = 10000.0
N_EXPERTS = 16
CAPACITY_FACTOR = 2
EXPERT_FF = 2048
EPS = 1e-6

F32 = jnp.float32
BF16 = jnp.bfloat16

VMEM_LIMIT_BYTES = 52 * 1024 * 1024
SUB_ROWS = 256
LANES = 128
SLABS = D_MODEL // LANES

K_OFF = D_ATTN
V_OFF = D_ATTN + D_KV
GM_OFF = D_ATTN + 2 * D_KV

Q_SCALE = math.log2(math.e) * HEAD_DIM ** -0.5


def _params(*sem):
    return pltpu.CompilerParams(dimension_semantics=sem, vmem_limit_bytes=VMEM_LIMIT_BYTES)


def _dot(a, b):
    return jnp.dot(a, b, preferred_element_type=F32)


def _dot_nt(a, b):
    return lax.dot_general(a, b, (((1,), (1,)), ((), ())), preferred_element_type=F32)


def _rms(v):
    return lax.rsqrt(jnp.sum(v * v, axis=-1, keepdims=True) * (1.0 / v.shape[-1]) + EPS)


def _modulate(x, g, shift, scale):
    return ((x * _rms(x)) * g) * (1.0 + scale) + shift


def _seg_sum(v, bd):
    w = bd.shape[0]
    outs = []
    for j in range(v.shape[1] // w):
        c = v[:, j * w:(j + 1) * w]
        hi = c.astype(BF16)
        lo = (c - hi.astype(F32)).astype(BF16)
        outs.append(_dot(hi, bd) + _dot(lo, bd))
    return outs[0] if len(outs) == 1 else jnp.concatenate(outs, axis=-1)


def _head_norm(z, bd, gain):
    ss = _seg_sum(z * z, bd)
    return (z * lax.rsqrt(ss * (1.0 / HEAD_DIM) + EPS)) * gain


def _rope(v, cos, sin_signed):
    w = v.shape[-1]
    lane = lax.broadcasted_iota(jnp.int32, v.shape, 1)
    partner = jnp.where((lane & (HEAD_DIM // 2)) == 0,
                        pltpu.roll(v, w - HEAD_DIM // 2, axis=1),
                        pltpu.roll(v, HEAD_DIM // 2, axis=1))
    return v * cos + partner * sin_signed


def _gelu(z):
    return 0.5 * z * (1.0 + lax.erf(z * math.sqrt(0.5)))


def _adaln_kernel(c_ref, w_ref, b_ref, o_ref):
    c = c_ref[...]
    s = c / (1.0 + jnp.exp(-c))
    o_ref[...] = _dot(s.astype(BF16), w_ref[...].astype(BF16)) + b_ref[...]


def _adaln(cvecs, w_mod, b_mod):
    rows = cvecs.shape[0]
    n_out = w_mod.shape[1]
    tn = 768
    return pl.pallas_call(
        _adaln_kernel,
        grid=(n_out // tn,),
        in_specs=[pl.BlockSpec((rows, D_MODEL), lambda j: (0, 0)),
                  pl.BlockSpec((D_MODEL, tn), lambda j: (0, j)),
                  pl.BlockSpec((1, tn), lambda j: (0, j))],
        out_specs=pl.BlockSpec((rows, tn), lambda j: (0, j)),
        out_shape=jax.ShapeDtypeStruct((rows, n_out), F32),
        compiler_params=_params("arbitrary"),
        name="adaln",
    )(cvecs, w_mod, b_mod.reshape(1, n_out))


def _inproj_kernel(x_ref, sh_ref, sc_ref, g_ref, w_ref, bd_ref, qg_ref, kg_ref, vg_ref,
                   cos_ref, sin_ref, q_ref, k_ref, vt_ref, u_ref, vv_ref):
    bd = bd_ref[...]
    for r0 in range(0, x_ref.shape[1], SUB_ROWS):
        rows = slice(r0, r0 + SUB_ROWS)
        h = _modulate(x_ref[0, rows, :], g_ref[...], sh_ref[0], sc_ref[0])
        z = _dot(h.astype(BF16), w_ref[...])
        cos = cos_ref[rows, :]
        sin = sin_ref[rows, :]

        q = _rope(_head_norm(z[:, :D_ATTN], bd, qg_ref[...]), cos, sin)
        q_ref[0, rows, :] = (q * Q_SCALE).astype(BF16)

        k = _head_norm(z[:, K_OFF:K_OFF + D_KV], bd[:D_KV, :D_KV], kg_ref[...])
        k_ref[0, rows, :] = _rope(k, cos[:, :D_KV], sin[:, :D_KV]).astype(BF16)
        vt_ref[0, :, rows] = z[:, V_OFF:V_OFF + D_KV].T.astype(BF16)

        gz = _gelu(z[:, GM_OFF:])
        u_ref[0, rows, :] = gz[:, :D_GM]
        vv = gz[:, D_GM:]
        vv_ref[0, rows, :] = ((vv * _rms(vv)) * vg_ref[...]).astype(BF16)


def _inproj(x, sh1, sc1, g_mix, w_in, bd, qg, kg, vg, cos_t, sin_t, tile):
    b, n, _ = x.shape
    row = lambda i, j: (j, i, 0)
    per_b = lambda i, j: (j, 0, 0)
    const = lambda i, j: (0, 0)
    tab = lambda i, j: (i, 0)
    return pl.pallas_call(
        _inproj_kernel,
        grid=(n // tile, b),
        in_specs=[pl.BlockSpec((1, tile, D_MODEL), row),
                  pl.BlockSpec((1, 1, D_MODEL), per_b),
                  pl.BlockSpec((1, 1, D_MODEL), per_b),
                  pl.BlockSpec((1, D_MODEL), const),
                  pl.BlockSpec((D_MODEL, D_IN), const),
                  pl.BlockSpec(bd.shape, const),
                  pl.BlockSpec((1, D_ATTN), const),
                  pl.BlockSpec((1, D_KV), const),
                  pl.BlockSpec((1, D_GM), const),
                  pl.BlockSpec((tile, D_ATTN), tab),
                  pl.BlockSpec((tile, D_ATTN), tab)],
        out_specs=[pl.BlockSpec((1, tile, D_ATTN), row),
                   pl.BlockSpec((1, tile, D_KV), row),
                   pl.BlockSpec((1, D_KV, tile), lambda i, j: (j, 0, i)),
                   pl.BlockSpec((1, tile, D_GM), row),
                   pl.BlockSpec((1, tile, D_GM), row)],
        out_shape=[jax.ShapeDtypeStruct((b, n, D_ATTN), BF16),
                   jax.ShapeDtypeStruct((b, n, D_KV), BF16),
                   jax.ShapeDtypeStruct((b, D_KV, n), BF16),
                   jax.ShapeDtypeStruct((b, n, D_GM), F32),
                   jax.ShapeDtypeStruct((b, n, D_GM), BF16)],
        compiler_params=_params("arbitrary", "arbitrary"),
        name="inproj",
    )(x, sh1, sc1, g_mix, w_in, bd, qg, kg, vg, cos_t, sin_t)


def _ctxkv_kernel(x_ref, sh_ref, sc_ref, g_ref, w_ref, bd_ref, kg_ref, k_ref, vt_ref):
    h = _modulate(x_ref[0], g_ref[...], sh_ref[...], sc_ref[...])
    z = _dot(h.astype(BF16), w_ref[:, K_OFF:K_OFF + 2 * D_KV])
    k_ref[0] = _head_norm(z[:, :D_KV], bd_ref[:D_KV, :D_KV], kg_ref[...]).astype(BF16)
    vt_ref[0] = z[:, D_KV:].T.astype(BF16)


def _ctxkv(ctx, csh1, csc1, g_mix, w_in, bd, kg):
    b, n, _ = ctx.shape
    const = lambda i: (0, 0)
    row = lambda i: (i, 0, 0)
    return pl.pallas_call(
        _ctxkv_kernel,
        grid=(b,),
        in_specs=[pl.BlockSpec((1, n, D_MODEL), row),
                  pl.BlockSpec((1, D_MODEL), const),
                  pl.BlockSpec((1, D_MODEL), const),
                  pl.BlockSpec((1, D_MODEL), const),
                  pl.BlockSpec((D_MODEL, D_IN), const),
                  pl.BlockSpec(bd.shape, const),
                  pl.BlockSpec((1, D_KV), const)],
        out_specs=[pl.BlockSpec((1, n, D_KV), row),
                   pl.BlockSpec((1, D_KV, n), row)],
        out_shape=[jax.ShapeDtypeStruct((b, n, D_KV), BF16),
                   jax.ShapeDtypeStruct((b, D_KV, n), BF16)],
        compiler_params=_params("arbitrary"),
        name="ctxkv",
    )(ctx, csh1, csc1, g_mix, w_in, bd, kg)


ATTN_LOOKAHEAD = 4
BOUND_MARGIN = 1.02
MAX_SAFE_SCORE_BOUND = 60.0


def _head_slices(q_ref, kl_ref, kc_ref, h):
    kv = slice((h // Q_PER_KV) * HEAD_DIM, (h // Q_PER_KV + 1) * HEAD_DIM)
    qh = q_ref[0, :, h * HEAD_DIM:(h + 1) * HEAD_DIM]
    return qh, kl_ref[0, :, kv], kc_ref[0, :, kv]


def _weighted_values(vlt_ref, vct_ref, p_ref, h, buf, n_lat):
    kv = slice((h // Q_PER_KV) * HEAD_DIM, (h // Q_PER_KV + 1) * HEAD_DIM)
    return (_dot(vlt_ref[0, kv, :], p_ref[buf, :n_lat, :])
            + _dot(vct_ref[0, kv, :], p_ref[buf, n_lat:, :]))


def _attn_kernel(q_ref, kl_ref, kc_ref, vlt_ref, vct_ref, o_ref, st_ref, pt_ref):
    n_lat = kl_ref.shape[1]

    def scores(h):
        qh, kl, kc = _head_slices(q_ref, kl_ref, kc_ref, h)
        st_ref[h, :n_lat, :] = _dot_nt(kl, qh)
        st_ref[h, n_lat:, :] = _dot_nt(kc, qh)

    for h in range(ATTN_LOOKAHEAD):
        scores(h)
    outs = []
    for h in range(N_Q_HEADS):
        if h + ATTN_LOOKAHEAD < N_Q_HEADS:
            scores(h + ATTN_LOOKAHEAD)
        st = st_ref[h]
        pt = jnp.exp2(st - jnp.max(st, axis=0, keepdims=True))
        l = jnp.sum(pt, axis=0, keepdims=True)
        pt_ref[h % 2] = pt.astype(BF16)
        outs.append(_weighted_values(vlt_ref, vct_ref, pt_ref, h, h % 2, n_lat) / l)
    o_ref[0] = jnp.concatenate(outs, axis=0).T.astype(BF16)


def _attn_bounded_kernel(shift_ref, q_ref, kl_ref, kc_ref, vlt_ref, vct_ref, o_ref, pt_ref):
    n_lat = kl_ref.shape[1]
    shift = shift_ref[...]
    sums = [None] * N_Q_HEADS

    def probs(h):
        qh, kl, kc = _head_slices(q_ref, kl_ref, kc_ref, h)
        pl_t = jnp.exp2(_dot_nt(kl, qh) - shift)
        pc_t = jnp.exp2(_dot_nt(kc, qh) - shift)
        sums[h] = jnp.sum(pl_t, axis=0, keepdims=True) + jnp.sum(pc_t, axis=0, keepdims=True)
        pt_ref[h, :n_lat, :] = pl_t.astype(BF16)
        pt_ref[h, n_lat:, :] = pc_t.astype(BF16)

    for h in range(ATTN_LOOKAHEAD):
        probs(h)
    outs = []
    for h in range(N_Q_HEADS):
        if h + ATTN_LOOKAHEAD < N_Q_HEADS:
            probs(h + ATTN_LOOKAHEAD)
        outs.append(_weighted_values(vlt_ref, vct_ref, pt_ref, h, h, n_lat) / sums[h])
    o_ref[0] = jnp.concatenate(outs, axis=0).T.astype(BF16)


def _attn(q, kl, kc, vlt, vct, score_bound, tile_bounded, tile_exact):
    b, n, _ = q.shape
    n_ctx = kc.shape[1]
    nk = n + n_ctx
    per_b = lambda i, j: (i, 0, 0)

    def specs(tile):
        return dict(
            grid=(b, n // tile),
            out_specs=pl.BlockSpec((1, tile, D_ATTN), lambda i, j: (i, j, 0)),
            out_shape=jax.ShapeDtypeStruct((b, n, D_ATTN), BF16),
            compiler_params=_params("arbitrary", "arbitrary"))

    def qkv_specs(tile):
        return [pl.BlockSpec((1, tile, D_ATTN), lambda i, j: (i, j, 0)),
                pl.BlockSpec((1, n, D_KV), per_b),
                pl.BlockSpec((1, n_ctx, D_KV), per_b),
                pl.BlockSpec((1, D_KV, n), per_b),
                pl.BlockSpec((1, D_KV, n_ctx), per_b)]

    def exact_max(score_bound, *qkv):
        t = tile_exact
        return pl.pallas_call(
            _attn_kernel, in_specs=qkv_specs(t),
            scratch_shapes=[pltpu.VMEM((N_Q_HEADS, nk, t), F32), pltpu.VMEM((2, nk, t), BF16)],
            name="attn", **specs(t),
        )(*qkv)

    def bounded(score_bound, *qkv):
        t = tile_bounded
        return pl.pallas_call(
            _attn_bounded_kernel, in_specs=[pl.BlockSpec((1, t), lambda i, j: (0, 0))] + qkv_specs(t),
            scratch_shapes=[pltpu.VMEM((N_Q_HEADS, nk, t), BF16)], name="attn_bounded", **specs(t),
        )(jnp.full((1, t), score_bound, F32), *qkv)

    return lax.cond(score_bound <= MAX_SAFE_SCORE_BOUND, bounded, exact_max,
                    score_bound, q, kl, kc, vlt, vct)


def _mixout_kernel(o_ref, u_ref, vv_ref, x_ref, ws_ref, bs_ref, wout_ref, gt_ref, g_ref,
                   sh_ref, sc_ref, wr_ref, xn_ref, h_ref, aff_ref):
    tile = o_ref.shape[1]
    bs = bs_ref[...]
    gms = []
    for c in range(tile // CHUNK):
        vc = vv_ref[0, c * CHUNK:(c + 1) * CHUNK, :]
        mixed = jnp.concatenate(
            [_dot(ws_ref[g], vc[:, g * GM_GROUP_DIM:(g + 1) * GM_GROUP_DIM]) for g in range(N_GM_GROUPS)],
            axis=-1)
        gms.append(u_ref[0, c * CHUNK:(c + 1) * CHUNK, :] * (mixed + bs))
    gm = jnp.concatenate(gms, axis=0).astype(BF16)

    proj = _dot(o_ref[0], wout_ref[:D_ATTN, :]) + _dot(gm, wout_ref[D_ATTN:, :])
    xn = x_ref[0] + gt_ref[0] * proj
    xn_ref[0] = xn

    h = _modulate(xn, g_ref[...], sh_ref[0], sc_ref[0]).astype(BF16)
    h_ref[0] = h
    logits = _dot_nt(wr_ref[...], h)
    e = jnp.exp(logits - jnp.max(logits, axis=0, keepdims=True))
    aff_ref[0] = e / jnp.sum(e, axis=0, keepdims=True)


def _mixout(o, u, vvn, x, w_s, bs, w_out, gt1, g_ffn, sh2, sc2, w_rt, tile):
    b, n, _ = x.shape
    row = lambda i, j: (i, j, 0)
    per_b = lambda i, j: (i, 0, 0)
    const2 = lambda i, j: (0, 0)
    return pl.pallas_call(
        _mixout_kernel,
        grid=(b, n // tile),
        in_specs=[pl.BlockSpec((1, tile, D_ATTN), row),
                  pl.BlockSpec((1, tile, D_GM), row),
                  pl.BlockSpec((1, tile, D_GM), row),
                  pl.BlockSpec((1, tile, D_MODEL), row),
                  pl.BlockSpec((N_GM_GROUPS, CHUNK, CHUNK), lambda i, j: (0, 0, 0)),
                  pl.BlockSpec((CHUNK, D_GM), const2),
                  pl.BlockSpec((D_ATTN + D_GM, D_MODEL), const2),
                  pl.BlockSpec((1, 1, D_MODEL), per_b),
                  pl.BlockSpec((1, D_MODEL), const2),
                  pl.BlockSpec((1, 1, D_MODEL), per_b),
                  pl.BlockSpec((1, 1, D_MODEL), per_b),
                  pl.BlockSpec((N_EXPERTS, D_MODEL), const2)],
        out_specs=[pl.BlockSpec((1, tile, D_MODEL), row),
                   pl.BlockSpec((1, tile, D_MODEL), row),
                   pl.BlockSpec((1, N_EXPERTS, tile), lambda i, j: (i, 0, j))],
        out_shape=[jax.ShapeDtypeStruct((b, n, D_MODEL), F32),
                   jax.ShapeDtypeStruct((b, n, D_MODEL), BF16),
                   jax.ShapeDtypeStruct((b, N_EXPERTS, n), F32)],
        compiler_params=_params("arbitrary", "arbitrary"),
        name="mixout",
    )(o, u, vvn, x, w_s, bs, w_out, gt1, g_ffn, sh2, sc2, w_rt)


def _cumsum_lanes(m, tri):
    off = jnp.zeros((m.shape[0], 1), F32)
    outs = []
    for blk in range(m.shape[1] // 128):
        mb = m[:, blk * 128:(blk + 1) * 128]
        outs.append(_dot(mb.astype(BF16), tri) + off)
        off = off + jnp.sum(mb, axis=-1, keepdims=True)
    return jnp.concatenate(outs, axis=-1)


REFINE_STEPS = 32


def _route_kernel(cap, aff_ref, tri_ref, pos_ref):
    a = aff_ref[...]
    capf = float(cap)

    def count_ge(t):
        return jnp.sum(jnp.where(a >= t, 1.0, 0.0), axis=-1, keepdims=True)

    def bit_step(i, prefix):
        cand = prefix | jnp.left_shift(jnp.int32(1), 30 - i)
        return jnp.where(count_ge(pltpu.bitcast(cand, F32)) >= capf, cand, prefix)

    prefix = lax.fori_loop(0, 31, bit_step, jnp.zeros((a.shape[0], 1), jnp.int32))
    lo = pltpu.bitcast(prefix, F32)
    hi = pltpu.bitcast(prefix + 1, F32)

    def refine(_, carry):
        lo, hi = carry
        mid = 0.5 * lo + 0.5 * hi
        ok = count_ge(mid) >= capf
        return jnp.where(ok, mid, lo), jnp.where(ok, hi, mid)

    lo, hi = lax.fori_loop(0, REFINE_STEPS, refine, (lo, hi))
    thr = jnp.min(jnp.where(a >= lo, a, jnp.inf), axis=-1, keepdims=True)

    gt = jnp.where(a > thr, 1.0, 0.0)
    eq = jnp.where(a == thr, 1.0, 0.0)
    need = capf - jnp.sum(gt, axis=-1, keepdims=True)
    tri = tri_ref[...]
    sel = gt + eq * jnp.where(_cumsum_lanes(eq, tri) <= need, 1.0, 0.0)
    slot = _cumsum_lanes(sel, tri) - 1.0
    pos_ref[...] = jnp.where(sel > 0.0, slot, -1.0).astype(jnp.int32)


def _route(aff_t, tri, cap, rows):
    b, e, n = aff_t.shape
    pos = pl.pallas_call(
        functools.partial(_route_kernel, cap),
        grid=(b * e // rows,),
        in_specs=[pl.BlockSpec((rows, n), lambda i: (i, 0)),
                  pl.BlockSpec((128, 128), lambda i: (0, 0))],
        out_specs=pl.BlockSpec((rows, n), lambda i: (i, 0)),
        out_shape=jax.ShapeDtypeStruct((b * e, n), jnp.int32),
        compiler_params=_params("arbitrary"),
        name="route",
    )(aff_t.reshape(b * e, n), tri)
    return pos.reshape(b, e, n)


GATHER_EXPERTS = 8


def _gather_kernel(cap, pos_ref, aff_ref, h_ref, xe_ref, gate_ref, tok_ref):
    n = pos_ref.shape[-1]
    slot = lax.broadcasted_iota(jnp.int32, (cap, n), 0)
    tok = (lax.broadcasted_iota(jnp.int32, (cap, n), 1) * SLABS).astype(F32)
    for x in range(GATHER_EXPERTS):
        hit = pos_ref[0, x:x + 1, :] == slot
        xe_ref[x] = _dot(jnp.where(hit, 1.0, 0.0).astype(BF16), h_ref[0]).astype(BF16)
        gate_ref[x] = jnp.sum(jnp.where(hit, aff_ref[0, x:x + 1, :], 0.0), axis=-1, keepdims=True)
        tok_col = jnp.sum(jnp.where(hit, tok, 0.0), axis=-1, keepdims=True)
        tok_row = jnp.broadcast_to(tok_col, (cap, LANES)).T[0:1, :]
        tok_ref[0, x:x + 1, :] = tok_row.astype(jnp.int32)


def _gather(pos_t, aff_t, h2, cap):
    b, e, n = pos_t.shape
    row = lambda i, j: (i, j, 0)
    slots = lambda i, j: (j, i, 0)
    return pl.pallas_call(
        functools.partial(_gather_kernel, cap),
        grid=(b, e // GATHER_EXPERTS),
        in_specs=[pl.BlockSpec((1, GATHER_EXPERTS, n), row),
                  pl.BlockSpec((1, GATHER_EXPERTS, n), row),
                  pl.BlockSpec((1, n, D_MODEL), lambda i, j: (i, 0, 0))],
        out_specs=[pl.BlockSpec((GATHER_EXPERTS, cap, D_MODEL), slots),
                   pl.BlockSpec((GATHER_EXPERTS, cap, 1), slots),
                   pl.BlockSpec((1, GATHER_EXPERTS, cap), lambda i, j: (i, j, 0))],
        out_shape=[jax.ShapeDtypeStruct((e, b * cap, D_MODEL), BF16),
                   jax.ShapeDtypeStruct((e, b * cap, 1), F32),
                   jax.ShapeDtypeStruct((b, e, cap), jnp.int32)],
        compiler_params=_params("arbitrary", "arbitrary"),
        name="gather",
    )(pos_t, aff_t, h2)


FF_CHUNK = 512


def _ffn_kernel(xe_ref, gate_ref, w1_ref, w3_ref, w2_ref, y_ref, w1a, w3a, w2a, w1b, w3b, w2b):
    i = pl.program_id(0)
    j = pl.program_id(1)
    rows13 = w1_ref.shape[1]
    rows2 = w2_ref.shape[1]
    tile = xe_ref.shape[1]
    sets = ((w1a, w3a, w2a), (w1b, w3b, w2b))

    def stage(dst):
        w1s, w3s, w2s = dst
        w1s[pl.ds(j * rows13, rows13), :] = w1_ref[0].astype(BF16)
        w3s[pl.ds(j * rows13, rows13), :] = w3_ref[0].astype(BF16)
        w2s[pl.ds(j * rows2, rows2), :] = w2_ref[0].astype(BF16)

    def compute(src):
        w1s, w3s, w2s = src
        xe = xe_ref[0]
        acc = jnp.zeros((tile, D_MODEL), F32)
        for c in range(EXPERT_FF // FF_CHUNK):
            cols = slice(c * FF_CHUNK, (c + 1) * FF_CHUNK)
            a = _dot(xe, w1s[:, cols])
            g = _dot(xe, w3s[:, cols])
            hid = (a / (1.0 + jnp.exp(-a))) * g
            acc = acc + _dot(hid.astype(BF16), w2s[cols, :])
        y = acc * gate_ref[0]
        for s in range(SLABS):
            y_ref[0, pl.ds(s, tile, stride=SLABS), :] = y[:, s * LANES:(s + 1) * LANES]

    @pl.when(i == 0)
    def _():
        stage(sets[0])
        y_ref[...] = jnp.zeros_like(y_ref)

    for parity in (0, 1):
        @pl.when((i > 0) & (i % 2 == parity))
        def _():
            compute(sets[1 - parity])
            stage(sets[parity])


def _ffn(xe, gate, w1, w3, w2, tile):
    e, m, _ = xe.shape
    steps = m // tile
    cur = lambda i, j: (jnp.maximum(i - 1, 0), j, 0)
    nxt = lambda i, j: (jnp.minimum(i, e - 1), j, 0)
    return pl.pallas_call(
        _ffn_kernel,
        grid=(e + 1, steps),
        in_specs=[pl.BlockSpec((1, tile, D_MODEL), cur),
                  pl.BlockSpec((1, tile, 1), cur),
                  pl.BlockSpec((1, D_MODEL // steps, EXPERT_FF), nxt),
                  pl.BlockSpec((1, D_MODEL // steps, EXPERT_FF), nxt),
                  pl.BlockSpec((1, EXPERT_FF // steps, D_MODEL), nxt)],
        out_specs=pl.BlockSpec((1, tile * SLABS, LANES), lambda i, j: (jnp.where(i == 0, e, i - 1), j, 0)),
        out_shape=jax.ShapeDtypeStruct((e + 1, m * SLABS, LANES), F32),
        scratch_shapes=[pltpu.VMEM((D_MODEL, EXPERT_FF), BF16),
                        pltpu.VMEM((D_MODEL, EXPERT_FF), BF16),
                        pltpu.VMEM((EXPERT_FF, D_MODEL), BF16)] * 2,
        compiler_params=_params("arbitrary", "arbitrary"),
        name="ffn",
    )(xe, gate, w1, w3, w2)


SCATTER_UNROLL = 8
SCATTER_EXPERTS = 8
FINAL_TILE = 1024


def _combine_kernel(cap, n_scatter, tok_ref, y_ref, xn_ref, gt_ref, g_ref, o_ref, acc_ref):
    b = pl.program_id(0)
    k = pl.program_id(1)

    @pl.when(k == 0)
    def _():
        acc_ref[...] = jnp.zeros_like(acc_ref)

    def scatter_expert(x):
        base = (b * n_scatter * SCATTER_EXPERTS + k * SCATTER_EXPERTS + x) * cap
        for g in range(cap // SCATTER_UNROLL):
            rows, vals = [], []
            for u in range(SCATTER_UNROLL):
                r = g * SCATTER_UNROLL + u
                row = pl.ds(pl.multiple_of(tok_ref[base + r], SLABS), SLABS)
                rows.append(row)
                vals.append(acc_ref[row, :] + y_ref[x, r * SLABS:(r + 1) * SLABS, :])
            for row, val in zip(rows, vals):
                acc_ref[row, :] = val

    @pl.when(k < n_scatter)
    def _():
        for x in range(SCATTER_EXPERTS):
            scatter_expert(x)

    @pl.when(k >= n_scatter)
    def _():
        start = (k - n_scatter) * (FINAL_TILE * SLABS)
        moe = jnp.concatenate(
            [acc_ref[pl.ds(start + s, FINAL_TILE, stride=SLABS), :] for s in range(SLABS)],
            axis=-1)
        z = xn_ref[0] + gt_ref[0] * moe
        o_ref[0] = (z * _rms(z)) * g_ref[...]


def _combine(y, tok, xn, gt2, g_final, cap):
    b, n, _ = xn.shape
    e = y.shape[0] - 1
    n_scatter = e // SCATTER_EXPERTS
    rows = lambda i, k, tok: (i, jnp.maximum(k - n_scatter, 0), 0)
    return pl.pallas_call(
        functools.partial(_combine_kernel, cap, n_scatter),
        grid_spec=pltpu.PrefetchScalarGridSpec(
            num_scalar_prefetch=1,
            grid=(b, n_scatter + n // FINAL_TILE),
            in_specs=[pl.BlockSpec((SCATTER_EXPERTS, cap * SLABS, LANES),
                                   lambda i, k, tok: (jnp.minimum(k, n_scatter - 1), i, 0)),
                      pl.BlockSpec((1, FINAL_TILE, D_MODEL), rows),
                      pl.BlockSpec((1, 1, D_MODEL), lambda i, k, tok: (i, 0, 0)),
                      pl.BlockSpec((1, D_MODEL), lambda i, k, tok: (0, 0))],
            out_specs=pl.BlockSpec((1, FINAL_TILE, D_MODEL), rows),
            scratch_shapes=[pltpu.VMEM((n * SLABS, LANES), F32)]),
        out_shape=jax.ShapeDtypeStruct((b, n, D_MODEL), F32),
        compiler_params=_params("arbitrary", "arbitrary"),
        name="combine",
    )(tok, y, xn, gt2, g_final)


def _rope_tables(n):
    pos = jnp.arange(n)
    n_freq = HEAD_DIM // 4
    inv = ROPE_THETA ** (-jnp.arange(n_freq, dtype=F32) / n_freq)
    ang = jnp.concatenate([(pos // GRID_W)[:, None].astype(F32) * inv,
                           (pos % GRID_W)[:, None].astype(F32) * inv], axis=-1)
    cos, sin = jnp.cos(ang), jnp.sin(ang)
    cos_t = jnp.tile(jnp.concatenate([cos, cos], axis=-1), (1, N_Q_HEADS))
    sin_t = jnp.tile(jnp.concatenate([-sin, sin], axis=-1), (1, N_Q_HEADS))
    return cos_t, sin_t


def _block_diag_ones(width, block):
    i = jnp.arange(width) // block
    return (i[:, None] == i[None, :]).astype(BF16)


def kernel(x, c, ctx, c_ctx, w_mod, b_mod, g_mix, g_ffn, w_in, q_gain, k_gain, v_gain, w_s, b_s,
           w_out, w_router, w1, w3, w2, g_final):
    assert w_mod.shape[0] == 1, "single-layer kernel"
    b, n, d = x.shape
    cap = CAPACITY_FACTOR * n // N_EXPERTS

    rows = -(-(b + 1) // 8) * 8
    cvecs = jnp.concatenate([c, c_ctx[None, :], jnp.zeros((rows - b - 1, d), F32)], axis=0)
    mods = _adaln(cvecs, w_mod[0], b_mod[0])
    sh1, sc1, gt1, sh2, sc2, gt2 = [m[:b, None, :] for m in jnp.split(mods, 6, axis=-1)]
    csh1, csc1 = mods[b:b + 1, :d], mods[b:b + 1, d:2 * d]

    cos_t, sin_t = _rope_tables(n)
    bd = _block_diag_ones(256, HEAD_DIM)
    w_in_b = w_in[0].astype(BF16)
    g_mix2 = g_mix[0][None, :]
    qg = jnp.tile(q_gain[0], N_Q_HEADS)[None, :]
    kg = jnp.tile(k_gain[0], N_KV_HEADS)[None, :]

    q, kl, vlt, u, vvn = _inproj(x, sh1, sc1, g_mix2, w_in_b, bd, qg, kg, v_gain[0][None, :],
                                cos_t, sin_t, tile=512)
    kc, vct = _ctxkv(ctx, csh1, csc1, g_mix2, w_in_b, bd, kg)
    score_bound = (BOUND_MARGIN * HEAD_DIM * Q_SCALE) * jnp.max(jnp.abs(q_gain[0])) * jnp.max(jnp.abs(k_gain[0]))
    o = _attn(q, kl, kc, vlt, vct, score_bound, tile_bounded=1024, tile_exact=512)

    bs = jnp.repeat(b_s[0].T, GM_GROUP_DIM, axis=1)
    xn, h2, aff_t = _mixout(o, u, vvn, x, w_s[0].astype(BF16), bs, w_out[0].astype(BF16), gt1,
                            g_ffn[0][None, :], sh2, sc2, w_router[0].T.astype(BF16), tile=512)

    tri = (jnp.arange(128)[:, None] <= jnp.arange(128)[None, :]).astype(BF16)
    pos_t = _route(aff_t, tri, cap, rows=64)
    xe, gate, tok = _gather(pos_t, aff_t, h2, cap)
    y = _ffn(xe, gate, w1[0], w3[0], w2[0], tile=512)
    return _combine(y, tok.reshape(-1), xn, gt2, g_final[None, :], cap)
```

```python
import functools
import math

import jax
import jax.numpy as jnp
from jax import lax
from jax.experimental import pallas as pl
from jax.experimental.pallas import tpu as pltpu

D_MODEL = 1024
CTX_LEN = 256
GRID_W = 64
HEAD_DIM = 64
N_Q_HEADS = 8
N_KV_HEADS = 2
Q_PER_KV = N_Q_HEADS // N_KV_HEADS
D_ATTN = N_Q_HEADS * HEAD_DIM
D_KV = N_KV_HEADS * HEAD_DIM
N_GM_GROUPS = 8
GM_GROUP_DIM = 64
D_GM = N_GM_GROUPS * GM_GROUP_DIM
CHUNK = 128
D_IN = D_ATTN + 2 * D_KV + 2 * D_GM
ROPE_THETA = 10000.0
N_EXPERTS = 16
CAPACITY_FACTOR = 2
EXPERT_FF = 2048
EPS = 1e-6

F32 = jnp.float32
BF16 = jnp.bfloat16

VMEM_LIMIT_BYTES = 52 * 1024 * 1024
SUB_ROWS = 256
LANES = 128
SLABS = D_MODEL // LANES

K_OFF = D_ATTN
V_OFF = D_ATTN + D_KV
GM_OFF = D_ATTN + 2 * D_KV

Q_SCALE = math.log2(math.e) * HEAD_DIM ** -0.5


def _params(*sem):
    return pltpu.CompilerParams(dimension_semantics=sem, vmem_limit_bytes=VMEM_LIMIT_BYTES)


def _dot(a, b):
    return jnp.dot(a, b, preferred_element_type=F32)


def _dot_nt(a, b):
    return lax.dot_general(a, b, (((1,), (1,)), ((), ())), preferred_element_type=F32)


def _rms(v):
    return lax.rsqrt(jnp.sum(v * v, axis=-1, keepdims=True) * (1.0 / v.shape[-1]) + EPS)


def _modulate(x, g, shift, scale):
    return ((x * _rms(x)) * g) * (1.0 + scale) + shift


def _seg_sum(v, bd):
    w = bd.shape[0]
    outs = []
    for j in range(v.shape[1] // w):
        c = v[:, j * w:(j + 1) * w]
        hi = c.astype(BF16)
        lo = (c - hi.astype(F32)).astype(BF16)
        outs.append(_dot(hi, bd) + _dot(lo, bd))
    return outs[0] if len(outs) == 1 else jnp.concatenate(outs, axis=-1)


def _head_norm(z, bd, gain):
    ss = _seg_sum(z * z, bd)
    return (z * lax.rsqrt(ss * (1.0 / HEAD_DIM) + EPS)) * gain


def _rope(v, cos, sin_signed):
    w = v.shape[-1]
    lane = lax.broadcasted_iota(jnp.int32, v.shape, 1)
    partner = jnp.where((lane & (HEAD_DIM // 2)) == 0,
                        pltpu.roll(v, w - HEAD_DIM // 2, axis=1),
                        pltpu.roll(v, HEAD_DIM // 2, axis=1))
    return v * cos + partner * sin_signed


def _gelu(z):
    return 0.5 * z * (1.0 + lax.erf(z * math.sqrt(0.5)))


def _adaln_kernel(c_ref, w_ref, b_ref, o_ref):
    c = c_ref[...]
    s = c / (1.0 + jnp.exp(-c))
    o_ref[...] = _dot(s.astype(BF16), w_ref[...].astype(BF16)) + b_ref[...]


def _adaln(cvecs, w_mod, b_mod):
    rows = cvecs.shape[0]
    n_out = w_mod.shape[1]
    tn = 768
    return pl.pallas_call(
        _adaln_kernel,
        grid=(n_out // tn,),
        in_specs=[pl.BlockSpec((rows, D_MODEL), lambda j: (0, 0)),
                  pl.BlockSpec((D_MODEL, tn), lambda j: (0, j)),
                  pl.BlockSpec((1, tn), lambda j: (0, j))],
        out_specs=pl.BlockSpec((rows, tn), lambda j: (0, j)),
        out_shape=jax.ShapeDtypeStruct((rows, n_out), F32),
        compiler_params=_params("arbitrary"),
        name="adaln",
    )(cvecs, w_mod, b_mod.reshape(1, n_out))


def _inproj_kernel(x_ref, sh_ref, sc_ref, g_ref, w_ref, bd_ref, qg_ref, kg_ref, vg_ref,
                   cos_ref, sin_ref, q_ref, k_ref, vt_ref, u_ref, vv_ref):
    bd = bd_ref[...]
    for r0 in range(0, x_ref.shape[1], SUB_ROWS):
        rows = slice(r0, r0 + SUB_ROWS)
        h = _modulate(x_ref[0, rows, :], g_ref[...], sh_ref[0], sc_ref[0])
        z = _dot(h.astype(BF16), w_ref[...])
        cos = cos_ref[rows, :]
        sin = sin_ref[rows, :]

        q = _rope(_head_norm(z[:, :D_ATTN], bd, qg_ref[...]), cos, sin)
        q_ref[0, rows, :] = (q * Q_SCALE).astype(BF16)

        k = _head_norm(z[:, K_OFF:K_OFF + D_KV], bd[:D_KV, :D_KV], kg_ref[...])
        k_ref[0, rows, :] = _rope(k, cos[:, :D_KV], sin[:, :D_KV]).astype(BF16)
        vt_ref[0, :, rows] = z[:, V_OFF:V_OFF + D_KV].T.astype(BF16)

        gz = _gelu(z[:, GM_OFF:])
        u_ref[0, rows, :] = gz[:, :D_GM]
        vv = gz[:, D_GM:]
        vv_ref[0, rows, :] = ((vv * _rms(vv)) * vg_ref[...]).astype(BF16)


def _inproj(x, sh1, sc1, g_mix, w_in, bd, qg, kg, vg, cos_t, sin_t, tile):
    b, n, _ = x.shape
    row = lambda i, j: (j, i, 0)
    per_b = lambda i, j: (j, 0, 0)
    const = lambda i, j: (0, 0)
    tab = lambda i, j: (i, 0)
    return pl.pallas_call(
        _inproj_kernel,
        grid=(n // tile, b),
        in_specs=[pl.BlockSpec((1, tile, D_MODEL), row),
                  pl.BlockSpec((1, 1, D_MODEL), per_b),
                  pl.BlockSpec((1, 1, D_MODEL), per_b),
                  pl.BlockSpec((1, D_MODEL), const),
                  pl.BlockSpec((D_MODEL, D_IN), const),
                  pl.BlockSpec(bd.shape, const),
                  pl.BlockSpec((1, D_ATTN), const),
                  pl.BlockSpec((1, D_KV), const),
                  pl.BlockSpec((1, D_GM), const),
                  pl.BlockSpec((tile, D_ATTN), tab),
                  pl.BlockSpec((tile, D_ATTN), tab)],
        out_specs=[pl.BlockSpec((1, tile, D_ATTN), row),
                   pl.BlockSpec((1, tile, D_KV), row),
                   pl.BlockSpec((1, D_KV, tile), lambda i, j: (j, 0, i)),
                   pl.BlockSpec((1, tile, D_GM), row),
                   pl.BlockSpec((1, tile, D_GM), row)],
        out_shape=[jax.ShapeDtypeStruct((b, n, D_ATTN), BF16),
                   jax.ShapeDtypeStruct((b, n, D_KV), BF16),
                   jax.ShapeDtypeStruct((b, D_KV, n), BF16),
                   jax.ShapeDtypeStruct((b, n, D_GM), F32),
                   jax.ShapeDtypeStruct((b, n, D_GM), BF16)],
        compiler_params=_params("arbitrary", "arbitrary"),
        name="inproj",
    )(x, sh1, sc1, g_mix, w_in, bd, qg, kg, vg, cos_t, sin_t)


def _ctxkv_kernel(x_ref, sh_ref, sc_ref, g_ref, w_ref, bd_ref, kg_ref, k_ref, vt_ref):
    for s in range(x_ref.shape[0]):
        h = _modulate(x_ref[s], g_ref[...], sh_ref[...], sc_ref[...])
        z = _dot(h.astype(BF16), w_ref[:, K_OFF:K_OFF + 2 * D_KV])
        k_ref[s] = _head_norm(z[:, :D_KV], bd_ref[:D_KV, :D_KV], kg_ref[...]).astype(BF16)
        vt_ref[s] = z[:, D_KV:].T.astype(BF16)


CTX_SAMPLES = 4


def _ctxkv(ctx, csh1, csc1, g_mix, w_in, bd, kg):
    b, n, _ = ctx.shape
    const = lambda i: (0, 0)
    row = lambda i: (i, 0, 0)
    return pl.pallas_call(
        _ctxkv_kernel,
        grid=(b // CTX_SAMPLES,),
        in_specs=[pl.BlockSpec((CTX_SAMPLES, n, D_MODEL), row),
                  pl.BlockSpec((1, D_MODEL), const),
                  pl.BlockSpec((1, D_MODEL), const),
                  pl.BlockSpec((1, D_MODEL), const),
                  pl.BlockSpec((D_MODEL, D_IN), const),
                  pl.BlockSpec(bd.shape, const),
                  pl.BlockSpec((1, D_KV), const)],
        out_specs=[pl.BlockSpec((CTX_SAMPLES, n, D_KV), row),
                   pl.BlockSpec((CTX_SAMPLES, D_KV, n), row)],
        out_shape=[jax.ShapeDtypeStruct((b, n, D_KV), BF16),
                   jax.ShapeDtypeStruct((b, D_KV, n), BF16)],
        compiler_params=_params("arbitrary"),
        name="ctxkv",
    )(ctx, csh1, csc1, g_mix, w_in, bd, kg)


ATTN_LOOKAHEAD = 4
BOUND_MARGIN = 1.02
MAX_SAFE_SCORE_BOUND = 60.0


def _head_slices(q_ref, kl_ref, kc_ref, h):
    kv = slice((h // Q_PER_KV) * HEAD_DIM, (h // Q_PER_KV + 1) * HEAD_DIM)
    qh = q_ref[0, :, h * HEAD_DIM:(h + 1) * HEAD_DIM]
    return qh, kl_ref[0, :, kv], kc_ref[0, :, kv]


def _weighted_values(vlt_ref, vct_ref, p_ref, h, buf, n_lat):
    kv = slice((h // Q_PER_KV) * HEAD_DIM, (h // Q_PER_KV + 1) * HEAD_DIM)
    return (_dot(vlt_ref[0, kv, :], p_ref[buf, :n_lat, :])
            + _dot(vct_ref[0, kv, :], p_ref[buf, n_lat:, :]))


def _attn_kernel(q_ref, kl_ref, kc_ref, vlt_ref, vct_ref, o_ref, st_ref, pt_ref):
    n_lat = kl_ref.shape[1]

    def scores(h):
        qh, kl, kc = _head_slices(q_ref, kl_ref, kc_ref, h)
        st_ref[h, :n_lat, :] = _dot_nt(kl, qh)
        st_ref[h, n_lat:, :] = _dot_nt(kc, qh)

    for h in range(ATTN_LOOKAHEAD):
        scores(h)
    outs = []
    for h in range(N_Q_HEADS):
        if h + ATTN_LOOKAHEAD < N_Q_HEADS:
            scores(h + ATTN_LOOKAHEAD)
        st = st_ref[h]
        pt = jnp.exp2(st - jnp.max(st, axis=0, keepdims=True))
        l = jnp.sum(pt, axis=0, keepdims=True)
        pt_ref[h % 2] = pt.astype(BF16)
        outs.append(_weighted_values(vlt_ref, vct_ref, pt_ref, h, h % 2, n_lat) / l)
    o_ref[0] = jnp.concatenate(outs, axis=0).T.astype(BF16)


def _attn_bounded_kernel(shift_ref, q_ref, kl_ref, kc_ref, vlt_ref, vct_ref, o_ref, pt_ref):
    n_lat = kl_ref.shape[1]
    shift = shift_ref[...]
    sums = [None] * N_Q_HEADS

    def probs(h):
        qh, kl, kc = _head_slices(q_ref, kl_ref, kc_ref, h)
        pl_t = jnp.exp2(_dot_nt(kl, qh) - shift)
        pc_t = jnp.exp2(_dot_nt(kc, qh) - shift)
        sums[h] = jnp.sum(pl_t, axis=0, keepdims=True) + jnp.sum(pc_t, axis=0, keepdims=True)
        pt_ref[h, :n_lat, :] = pl_t.astype(BF16)
        pt_ref[h, n_lat:, :] = pc_t.astype(BF16)

    for h in range(ATTN_LOOKAHEAD):
        probs(h)
    outs = []
    for h in range(N_Q_HEADS):
        if h + ATTN_LOOKAHEAD < N_Q_HEADS:
            probs(h + ATTN_LOOKAHEAD)
        outs.append(_weighted_values(vlt_ref, vct_ref, pt_ref, h, h, n_lat) / sums[h])
    o_ref[0] = jnp.concatenate(outs, axis=0).T.astype(BF16)


def _attn(q, kl, kc, vlt, vct, score_bound, tile_bounded, tile_exact):
    b, n, _ = q.shape
    n_ctx = kc.shape[1]
    nk = n + n_ctx
    per_b = lambda i, j: (i, 0, 0)

    def specs(tile):
        return dict(
            grid=(b, n // tile),
            out_specs=pl.BlockSpec((1, tile, D_ATTN), lambda i, j: (i, j, 0)),
            out_shape=jax.ShapeDtypeStruct((b, n, D_ATTN), BF16),
            compiler_params=_params("arbitrary", "arbitrary"))

    def qkv_specs(tile):
        return [pl.BlockSpec((1, tile, D_ATTN), lambda i, j: (i, j, 0)),
                pl.BlockSpec((1, n, D_KV), per_b),
                pl.BlockSpec((1, n_ctx, D_KV), per_b),
                pl.BlockSpec((1, D_KV, n), per_b),
                pl.BlockSpec((1, D_KV, n_ctx), per_b)]

    def exact_max(score_bound, *qkv):
        t = tile_exact
        return pl.pallas_call(
            _attn_kernel, in_specs=qkv_specs(t),
            scratch_shapes=[pltpu.VMEM((N_Q_HEADS, nk, t), F32), pltpu.VMEM((2, nk, t), BF16)],
            name="attn", **specs(t),
        )(*qkv)

    def bounded(score_bound, *qkv):
        t = tile_bounded
        return pl.pallas_call(
            _attn_bounded_kernel, in_specs=[pl.BlockSpec((1, t), lambda i, j: (0, 0))] + qkv_specs(t),
            scratch_shapes=[pltpu.VMEM((N_Q_HEADS, nk, t), BF16)], name="attn_bounded", **specs(t),
        )(jnp.full((1, t), score_bound, F32), *qkv)

    return lax.cond(score_bound <= MAX_SAFE_SCORE_BOUND, bounded, exact_max,
                    score_bound, q, kl, kc, vlt, vct)


def _mixout_kernel(o_ref, u_ref, vv_ref, x_ref, ws_ref, bs_ref, wout_ref, gt_ref, g_ref,
                   sh_ref, sc_ref, wr_ref, xn_ref, h_ref, aff_ref):
    tile = o_ref.shape[1]
    bs = bs_ref[...]
    gms = []
    for c in range(tile // CHUNK):
        vc = vv_ref[0, c * CHUNK:(c + 1) * CHUNK, :]
        mixed = jnp.concatenate(
            [_dot(ws_ref[g], vc[:, g * GM_GROUP_DIM:(g + 1) * GM_GROUP_DIM]) for g in range(N_GM_GROUPS)],
            axis=-1)
        gms.append(u_ref[0, c * CHUNK:(c + 1) * CHUNK, :] * (mixed + bs))
    gm = jnp.concatenate(gms, axis=0).astype(BF16)

    proj = _dot(o_ref[0], wout_ref[:D_ATTN, :]) + _dot(gm, wout_ref[D_ATTN:, :])
    xn = x_ref[0] + gt_ref[0] * proj
    xn_ref[0] = xn

    h = _modulate(xn, g_ref[...], sh_ref[0], sc_ref[0]).astype(BF16)
    h_ref[0] = h
    logits = _dot_nt(wr_ref[...], h)
    e = jnp.exp(logits - jnp.max(logits, axis=0, keepdims=True))
    aff_ref[0] = e / jnp.sum(e, axis=0, keepdims=True)


def _mixout(o, u, vvn, x, w_s, bs, w_out, gt1, g_ffn, sh2, sc2, w_rt, tile):
    b, n, _ = x.shape
    row = lambda i, j: (i, j, 0)
    per_b = lambda i, j: (i, 0, 0)
    const2 = lambda i, j: (0, 0)
    return pl.pallas_call(
        _mixout_kernel,
        grid=(b, n // tile),
        in_specs=[pl.BlockSpec((1, tile, D_ATTN), row),
                  pl.BlockSpec((1, tile, D_GM), row),
                  pl.BlockSpec((1, tile, D_GM), row),
                  pl.BlockSpec((1, tile, D_MODEL), row),
                  pl.BlockSpec((N_GM_GROUPS, CHUNK, CHUNK), lambda i, j: (0, 0, 0)),
                  pl.BlockSpec((CHUNK, D_GM), const2),
                  pl.BlockSpec((D_ATTN + D_GM, D_MODEL), const2),
                  pl.BlockSpec((1, 1, D_MODEL), per_b),
                  pl.BlockSpec((1, D_MODEL), const2),
                  pl.BlockSpec((1, 1, D_MODEL), per_b),
                  pl.BlockSpec((1, 1, D_MODEL), per_b),
                  pl.BlockSpec((N_EXPERTS, D_MODEL), const2)],
        out_specs=[pl.BlockSpec((1, tile, D_MODEL), row),
                   pl.BlockSpec((1, tile, D_MODEL), row),
                   pl.BlockSpec((1, N_EXPERTS, tile), lambda i, j: (i, 0, j))],
        out_shape=[jax.ShapeDtypeStruct((b, n, D_MODEL), F32),
                   jax.ShapeDtypeStruct((b, n, D_MODEL), BF16),
                   jax.ShapeDtypeStruct((b, N_EXPERTS, n), F32)],
        compiler_params=_params("arbitrary", "arbitrary"),
        name="mixout",
    )(o, u, vvn, x, w_s, bs, w_out, gt1, g_ffn, sh2, sc2, w_rt)


def _cumsum_lanes(m, tri):
    off = jnp.zeros((m.shape[0], 1), F32)
    outs = []
    for blk in range(m.shape[1] // 128):
        mb = m[:, blk * 128:(blk + 1) * 128]
        outs.append(_dot(mb.astype(BF16), tri) + off)
        off = off + jnp.sum(mb, axis=-1, keepdims=True)
    return jnp.concatenate(outs, axis=-1)


REFINE_STEPS = 32


def _route_kernel(cap, aff_ref, tri_ref, pos_ref):
    a = aff_ref[...]
    capf = float(cap)

    def count_ge(t):
        return jnp.sum(jnp.where(a >= t, 1.0, 0.0), axis=-1, keepdims=True)

    def bit_step(i, prefix):
        cand = prefix | jnp.left_shift(jnp.int32(1), 30 - i)
        return jnp.where(count_ge(pltpu.bitcast(cand, F32)) >= capf, cand, prefix)

    prefix = lax.fori_loop(0, 31, bit_step, jnp.zeros((a.shape[0], 1), jnp.int32))
    lo = pltpu.bitcast(prefix, F32)
    hi = pltpu.bitcast(prefix + 1, F32)

    def refine(_, carry):
        lo, hi = carry
        mid = 0.5 * lo + 0.5 * hi
        ok = count_ge(mid) >= capf
        return jnp.where(ok, mid, lo), jnp.where(ok, hi, mid)

    lo, hi = lax.fori_loop(0, REFINE_STEPS, refine, (lo, hi))
    thr = jnp.min(jnp.where(a >= lo, a, jnp.inf), axis=-1, keepdims=True)

    gt = jnp.where(a > thr, 1.0, 0.0)
    eq = jnp.where(a == thr, 1.0, 0.0)
    need = capf - jnp.sum(gt, axis=-1, keepdims=True)
    tri = tri_ref[...]
    sel = gt + eq * jnp.where(_cumsum_lanes(eq, tri) <= need, 1.0, 0.0)
    slot = _cumsum_lanes(sel, tri) - 1.0
    pos_ref[...] = jnp.where(sel > 0.0, slot, -1.0).astype(jnp.int32)


def _route(aff_t, tri, cap, rows):
    b, e, n = aff_t.shape
    pos = pl.pallas_call(
        functools.partial(_route_kernel, cap),
        grid=(b * e // rows,),
        in_specs=[pl.BlockSpec((rows, n), lambda i: (i, 0)),
                  pl.BlockSpec((128, 128), lambda i: (0, 0))],
        out_specs=pl.BlockSpec((rows, n), lambda i: (i, 0)),
        out_shape=jax.ShapeDtypeStruct((b * e, n), jnp.int32),
        compiler_params=_params("arbitrary"),
        name="route",
    )(aff_t.reshape(b * e, n), tri)
    return pos.reshape(b, e, n)


GATHER_EXPERTS = 8


def _gather_kernel(cap, pos_ref, aff_ref, h_ref, xe_ref, gate_ref, tok_ref):
    n = pos_ref.shape[-1]
    slot = lax.broadcasted_iota(jnp.int32, (cap, n), 0)
    tok = (lax.broadcasted_iota(jnp.int32, (cap, n), 1) * SLABS).astype(F32)
    for x in range(GATHER_EXPERTS):
        hit = pos_ref[0, x:x + 1, :] == slot
        xe_ref[x] = _dot(jnp.where(hit, 1.0, 0.0).astype(BF16), h_ref[0]).astype(BF16)
        gate_ref[x] = jnp.sum(jnp.where(hit, aff_ref[0, x:x + 1, :], 0.0), axis=-1, keepdims=True)
        tok_col = jnp.sum(jnp.where(hit, tok, 0.0), axis=-1, keepdims=True)
        tok_row = jnp.broadcast_to(tok_col, (cap, LANES)).T[0:1, :]
        tok_ref[0, x:x + 1, :] = tok_row.astype(jnp.int32)


def _gather(pos_t, aff_t, h2, cap):
    b, e, n = pos_t.shape
    row = lambda i, j: (i, j, 0)
    slots = lambda i, j: (j, i, 0)
    return pl.pallas_call(
        functools.partial(_gather_kernel, cap),
        grid=(b, e // GATHER_EXPERTS),
        in_specs=[pl.BlockSpec((1, GATHER_EXPERTS, n), row),
                  pl.BlockSpec((1, GATHER_EXPERTS, n), row),
                  pl.BlockSpec((1, n, D_MODEL), lambda i, j: (i, 0, 0))],
        out_specs=[pl.BlockSpec((GATHER_EXPERTS, cap, D_MODEL), slots),
                   pl.BlockSpec((GATHER_EXPERTS, cap, 1), slots),
                   pl.BlockSpec((1, GATHER_EXPERTS, cap), lambda i, j: (i, j, 0))],
        out_shape=[jax.ShapeDtypeStruct((e, b * cap, D_MODEL), BF16),
                   jax.ShapeDtypeStruct((e, b * cap, 1), F32),
                   jax.ShapeDtypeStruct((b, e, cap), jnp.int32)],
        compiler_params=_params("arbitrary", "arbitrary"),
        name="gather",
    )(pos_t, aff_t, h2)


FF_CHUNK = 512


def _ffn_kernel(xe_ref, gate_ref, w1_ref, w3_ref, w2_ref, y_ref, w1a, w3a, w2a, w1b, w3b, w2b):
    i = pl.program_id(0)
    j = pl.program_id(1)
    rows13 = w1_ref.shape[1]
    rows2 = w2_ref.shape[1]
    tile = xe_ref.shape[1]
    sets = ((w1a, w3a, w2a), (w1b, w3b, w2b))

    def stage(dst):
        w1s, w3s, w2s = dst
        w1s[pl.ds(j * rows13, rows13), :] = w1_ref[0].astype(BF16)
        w3s[pl.ds(j * rows13, rows13), :] = w3_ref[0].astype(BF16)
        w2s[pl.ds(j * rows2, rows2), :] = w2_ref[0].astype(BF16)

    def compute(src):
        w1s, w3s, w2s = src
        xe = xe_ref[0]
        acc = jnp.zeros((tile, D_MODEL), F32)
        for c in range(EXPERT_FF // FF_CHUNK):
            cols = slice(c * FF_CHUNK, (c + 1) * FF_CHUNK)
            a = _dot(xe, w1s[:, cols])
            g = _dot(xe, w3s[:, cols])
            hid = (a / (1.0 + jnp.exp(-a))) * g
            acc = acc + _dot(hid.astype(BF16), w2s[cols, :])
        y = acc * gate_ref[0]
        for s in range(SLABS):
            y_ref[0, pl.ds(s, tile, stride=SLABS), :] = y[:, s * LANES:(s + 1) * LANES]

    @pl.when(i == 0)
    def _():
        stage(sets[0])
        y_ref[...] = jnp.zeros_like(y_ref)

    for parity in (0, 1):
        @pl.when((i > 0) & (i % 2 == parity))
        def _():
            compute(sets[1 - parity])
            stage(sets[parity])


def _ffn(xe, gate, w1, w3, w2, tile):
    e, m, _ = xe.shape
    steps = m // tile
    cur = lambda i, j: (jnp.maximum(i - 1, 0), j, 0)
    nxt = lambda i, j: (jnp.minimum(i, e - 1), j, 0)
    return pl.pallas_call(
        _ffn_kernel,
        grid=(e + 1, steps),
        in_specs=[pl.BlockSpec((1, tile, D_MODEL), cur),
                  pl.BlockSpec((1, tile, 1), cur),
                  pl.BlockSpec((1, D_MODEL // steps, EXPERT_FF), nxt),
                  pl.BlockSpec((1, D_MODEL // steps, EXPERT_FF), nxt),
                  pl.BlockSpec((1, EXPERT_FF // steps, D_MODEL), nxt)],
        out_specs=pl.BlockSpec((1, tile * SLABS, LANES), lambda i, j: (jnp.where(i == 0, e, i - 1), j, 0)),
        out_shape=jax.ShapeDtypeStruct((e + 1, m * SLABS, LANES), F32),
        scratch_shapes=[pltpu.VMEM((D_MODEL, EXPERT_FF), BF16),
                        pltpu.VMEM((D_MODEL, EXPERT_FF), BF16),
                        pltpu.VMEM((EXPERT_FF, D_MODEL), BF16)] * 2,
        compiler_params=_params("arbitrary", "arbitrary"),
        name="ffn",
    )(xe, gate, w1, w3, w2)


SCATTER_UNROLL = 8
SCATTER_EXPERTS = 8
FINAL_TILE = 1024


def _combine_kernel(cap, n_scatter, tok_ref, y_ref, xn_ref, gt_ref, g_ref, o_ref, acc_ref):
    b = pl.program_id(0)
    k = pl.program_id(1)

    @pl.when(k == 0)
    def _():
        acc_ref[...] = jnp.zeros_like(acc_ref)

    def scatter_expert(x):
        base = (b * n_scatter * SCATTER_EXPERTS + k * SCATTER_EXPERTS + x) * cap
        for g in range(cap // SCATTER_UNROLL):
            rows, vals = [], []
            for u in range(SCATTER_UNROLL):
                r = g * SCATTER_UNROLL + u
                row = pl.ds(pl.multiple_of(tok_ref[base + r], SLABS), SLABS)
                rows.append(row)
                vals.append(acc_ref[row, :] + y_ref[x, r * SLABS:(r + 1) * SLABS, :])
            for row, val in zip(rows, vals):
                acc_ref[row, :] = val

    @pl.when(k < n_scatter)
    def _():
        for x in range(SCATTER_EXPERTS):
            scatter_expert(x)

    @pl.when(k >= n_scatter)
    def _():
        start = (k - n_scatter) * (FINAL_TILE * SLABS)
        moe = jnp.concatenate(
            [acc_ref[pl.ds(start + s, FINAL_TILE, stride=SLABS), :] for s in range(SLABS)],
            axis=-1)
        z = xn_ref[0] + gt_ref[0] * moe
        o_ref[0] = (z * _rms(z)) * g_ref[...]


def _combine(y, tok, xn, gt2, g_final, cap):
    b, n, _ = xn.shape
    e = y.shape[0] - 1
    n_scatter = e // SCATTER_EXPERTS
    rows = lambda i, k, tok: (i, jnp.maximum(k - n_scatter, 0), 0)
    return pl.pallas_call(
        functools.partial(_combine_kernel, cap, n_scatter),
        grid_spec=pltpu.PrefetchScalarGridSpec(
            num_scalar_prefetch=1,
            grid=(b, n_scatter + n // FINAL_TILE),
            in_specs=[pl.BlockSpec((SCATTER_EXPERTS, cap * SLABS, LANES),
                                   lambda i, k, tok: (jnp.minimum(k, n_scatter - 1), i, 0)),
                      pl.BlockSpec((1, FINAL_TILE, D_MODEL), rows),
                      pl.BlockSpec((1, 1, D_MODEL), lambda i, k, tok: (i, 0, 0)),
                      pl.BlockSpec((1, D_MODEL), lambda i, k, tok: (0, 0))],
            out_specs=pl.BlockSpec((1, FINAL_TILE, D_MODEL), rows),
            scratch_shapes=[pltpu.VMEM((n * SLABS, LANES), F32)]),
        out_shape=jax.ShapeDtypeStruct((b, n, D_MODEL), F32),
        compiler_params=_params("arbitrary", "arbitrary"),
        name="combine",
    )(tok, y, xn, gt2, g_final)


def _rope_tables(n):
    pos = jnp.arange(n)
    n_freq = HEAD_DIM // 4
    inv = ROPE_THETA ** (-jnp.arange(n_freq, dtype=F32) / n_freq)
    ang = jnp.concatenate([(pos // GRID_W)[:, None].astype(F32) * inv,
                           (pos % GRID_W)[:, None].astype(F32) * inv], axis=-1)
    cos, sin = jnp.cos(ang), jnp.sin(ang)
    cos_t = jnp.tile(jnp.concatenate([cos, cos], axis=-1), (1, N_Q_HEADS))
    sin_t = jnp.tile(jnp.concatenate([-sin, sin], axis=-1), (1, N_Q_HEADS))
    return cos_t, sin_t


def _block_diag_ones(width, block):
    i = jnp.arange(width) // block
    return (i[:, None] == i[None, :]).astype(BF16)


def kernel(x, c, ctx, c_ctx, w_mod, b_mod, g_mix, g_ffn, w_in, q_gain, k_gain, v_gain, w_s, b_s,
           w_out, w_router, w1, w3, w2, g_final):
    assert w_mod.shape[0] == 1, "single-layer kernel"
    b, n, d = x.shape
    cap = CAPACITY_FACTOR * n // N_EXPERTS

    rows = -(-(b + 1) // 8) * 8
    cvecs = jnp.concatenate([c, c_ctx[None, :], jnp.zeros((rows - b - 1, d), F32)], axis=0)
    mods = _adaln(cvecs, w_mod[0], b_mod[0])
    sh1, sc1, gt1, sh2, sc2, gt2 = [m[:b, None, :] for m in jnp.split(mods, 6, axis=-1)]
    csh1, csc1 = mods[b:b + 1, :d], mods[b:b + 1, d:2 * d]

    cos_t, sin_t = _rope_tables(n)
    bd = _block_diag_ones(256, HEAD_DIM)
    w_in_b = w_in[0].astype(BF16)
    g_mix2 = g_mix[0][None, :]
    qg = jnp.tile(q_gain[0], N_Q_HEADS)[None, :]
    kg = jnp.tile(k_gain[0], N_KV_HEADS)[None, :]

    q, kl, vlt, u, vvn = _inproj(x, sh1, sc1, g_mix2, w_in_b, bd, qg, kg, v_gain[0][None, :],
                                cos_t, sin_t, tile=512)
    kc, vct = _ctxkv(ctx, csh1, csc1, g_mix2, w_in_b, bd, kg)
    score_bound = (BOUND_MARGIN * HEAD_DIM * Q_SCALE) * jnp.max(jnp.abs(q_gain[0])) * jnp.max(jnp.abs(k_gain[0]))
    o = _attn(q, kl, kc, vlt, vct, score_bound, tile_bounded=1024, tile_exact=512)

    bs = jnp.repeat(b_s[0].T, GM_GROUP_DIM, axis=1)
    xn, h2, aff_t = _mixout(o, u, vvn, x, w_s[0].astype(BF16), bs, w_out[0].astype(BF16), gt1,
                            g_ffn[0][None, :], sh2, sc2, w_router[0].T.astype(BF16), tile=512)

    tri = (jnp.arange(128)[:, None] <= jnp.arange(128)[None, :]).astype(BF16)
    pos_t = _route(aff_t, tri, cap, rows=256)
    xe, gate, tok = _gather(pos_t, aff_t, h2, cap)
    y = _ffn(xe, gate, w1[0], w3[0], w2[0], tile=512)
    return _combine(y, tok.reshape(-1), xn, gt2, g_final[None, :], cap)
```

```python
import functools
import math

import jax
import jax.numpy as jnp
from jax import lax
from jax.experimental import pallas as pl
from jax.experimental.pallas import tpu as pltpu

D_MODEL = 1024
CTX_LEN = 256
GRID_W = 64
HEAD_DIM = 64
N_Q_HEADS = 8
N_KV_HEADS = 2
Q_PER_KV = N_Q_HEADS // N_KV_HEADS
D_ATTN = N_Q_HEADS * HEAD_DIM
D_KV = N_KV_HEADS * HEAD_DIM
N_GM_GROUPS = 8
GM_GROUP_DIM = 64
D_GM = N_GM_GROUPS * GM_GROUP_DIM
CHUNK = 128
D_IN = D_ATTN + 2 * D_KV + 2 * D_GM
ROPE_THETA = 10000.0
N_EXPERTS = 16
CAPACITY_FACTOR = 2
EXPERT_FF = 2048
EPS = 1e-6

F32 = jnp.float32
BF16 = jnp.bfloat16

VMEM_LIMIT_BYTES = 52 * 1024 * 1024
SUB_ROWS = 256
LANES = 128
SLABS = D_MODEL // LANES

K_OFF = D_ATTN
V_OFF = D_ATTN + D_KV
GM_OFF = D_ATTN + 2 * D_KV

Q_SCALE = math.log2(math.e) * HEAD_DIM ** -0.5


def _params(*sem):
    return pltpu.CompilerParams(dimension_semantics=sem, vmem_limit_bytes=VMEM_LIMIT_BYTES)


def _dot(a, b):
    return jnp.dot(a, b, preferred_element_type=F32)


def _dot_nt(a, b):
    return lax.dot_general(a, b, (((1,), (1,)), ((), ())), preferred_element_type=F32)


def _rms(v):
    return lax.rsqrt(jnp.sum(v * v, axis=-1, keepdims=True) * (1.0 / v.shape[-1]) + EPS)


def _modulate(x, g, shift, scale):
    return ((x * _rms(x)) * g) * (1.0 + scale) + shift


def _seg_sum(v, bd):
    w = bd.shape[0]
    outs = []
    for j in range(v.shape[1] // w):
        c = v[:, j * w:(j + 1) * w]
        hi = c.astype(BF16)
        lo = (c - hi.astype(F32)).astype(BF16)
        outs.append(_dot(hi, bd) + _dot(lo, bd))
    return outs[0] if len(outs) == 1 else jnp.concatenate(outs, axis=-1)


def _head_norm(z, bd, gain):
    ss = _seg_sum(z * z, bd)
    return (z * lax.rsqrt(ss * (1.0 / HEAD_DIM) + EPS)) * gain


def _rope(v, cos, sin_signed):
    w = v.shape[-1]
    lane = lax.broadcasted_iota(jnp.int32, v.shape, 1)
    partner = jnp.where((lane & (HEAD_DIM // 2)) == 0,
                        pltpu.roll(v, w - HEAD_DIM // 2, axis=1),
                        pltpu.roll(v, HEAD_DIM // 2, axis=1))
    return v * cos + partner * sin_signed


def _gelu(z):
    return 0.5 * z * (1.0 + lax.erf(z * math.sqrt(0.5)))


def _adaln_kernel(c_ref, w_ref, b_ref, o_ref):
    c = c_ref[...]
    s = c / (1.0 + jnp.exp(-c))
    o_ref[...] = _dot(s.astype(BF16), w_ref[...].astype(BF16)) + b_ref[...]


def _adaln(cvecs, w_mod, b_mod):
    rows = cvecs.shape[0]
    n_out = w_mod.shape[1]
    tn = 768
    return pl.pallas_call(
        _adaln_kernel,
        grid=(n_out // tn,),
        in_specs=[pl.BlockSpec((rows, D_MODEL), lambda j: (0, 0)),
                  pl.BlockSpec((D_MODEL, tn), lambda j: (0, j)),
                  pl.BlockSpec((1, tn), lambda j: (0, j))],
        out_specs=pl.BlockSpec((rows, tn), lambda j: (0, j)),
        out_shape=jax.ShapeDtypeStruct((rows, n_out), F32),
        compiler_params=_params("arbitrary"),
        name="adaln",
    )(cvecs, w_mod, b_mod.reshape(1, n_out))


def _inproj_kernel(x_ref, sh_ref, sc_ref, g_ref, w_ref, bd_ref, qg_ref, kg_ref, vg_ref,
                   cos_ref, sin_ref, q_ref, k_ref, vt_ref, u_ref, vv_ref):
    bd = bd_ref[...]
    for r0 in range(0, x_ref.shape[1], SUB_ROWS):
        rows = slice(r0, r0 + SUB_ROWS)
        h = _modulate(x_ref[0, rows, :], g_ref[...], sh_ref[0], sc_ref[0])
        z = _dot(h.astype(BF16), w_ref[...])
        cos = cos_ref[rows, :]
        sin = sin_ref[rows, :]

        q = _rope(_head_norm(z[:, :D_ATTN], bd, qg_ref[...]), cos, sin)
        q_ref[0, rows, :] = (q * Q_SCALE).astype(BF16)

        k = _head_norm(z[:, K_OFF:K_OFF + D_KV], bd[:D_KV, :D_KV], kg_ref[...])
        k_ref[0, rows, :] = _rope(k, cos[:, :D_KV], sin[:, :D_KV]).astype(BF16)
        vt_ref[0, :, rows] = z[:, V_OFF:V_OFF + D_KV].T.astype(BF16)

        gz = _gelu(z[:, GM_OFF:])
        u_ref[0, rows, :] = gz[:, :D_GM]
        vv = gz[:, D_GM:]
        vv_ref[0, rows, :] = ((vv * _rms(vv)) * vg_ref[...]).astype(BF16)


def _inproj(x, sh1, sc1, g_mix, w_in, bd, qg, kg, vg, cos_t, sin_t, tile):
    b, n, _ = x.shape
    row = lambda i, j: (j, i, 0)
    per_b = lambda i, j: (j, 0, 0)
    const = lambda i, j: (0, 0)
    tab = lambda i, j: (i, 0)
    return pl.pallas_call(
        _inproj_kernel,
        grid=(n // tile, b),
        in_specs=[pl.BlockSpec((1, tile, D_MODEL), row),
                  pl.BlockSpec((1, 1, D_MODEL), per_b),
                  pl.BlockSpec((1, 1, D_MODEL), per_b),
                  pl.BlockSpec((1, D_MODEL), const),
                  pl.BlockSpec((D_MODEL, D_IN), const),
                  pl.BlockSpec(bd.shape, const),
                  pl.BlockSpec((1, D_ATTN), const),
                  pl.BlockSpec((1, D_KV), const),
                  pl.BlockSpec((1, D_GM), const),
                  pl.BlockSpec((tile, D_ATTN), tab),
                  pl.BlockSpec((tile, D_ATTN), tab)],
        out_specs=[pl.BlockSpec((1, tile, D_ATTN), row),
                   pl.BlockSpec((1, tile, D_KV), row),
                   pl.BlockSpec((1, D_KV, tile), lambda i, j: (j, 0, i)),
                   pl.BlockSpec((1, tile, D_GM), row),
                   pl.BlockSpec((1, tile, D_GM), row)],
        out_shape=[jax.ShapeDtypeStruct((b, n, D_ATTN), BF16),
                   jax.ShapeDtypeStruct((b, n, D_KV), BF16),
                   jax.ShapeDtypeStruct((b, D_KV, n), BF16),
                   jax.ShapeDtypeStruct((b, n, D_GM), F32),
                   jax.ShapeDtypeStruct((b, n, D_GM), BF16)],
        compiler_params=_params("arbitrary", "arbitrary"),
        name="inproj",
    )(x, sh1, sc1, g_mix, w_in, bd, qg, kg, vg, cos_t, sin_t)


def _ctxkv_kernel(x_ref, sh_ref, sc_ref, g_ref, w_ref, bd_ref, kg_ref, k_ref, vt_ref):
    for s in range(x_ref.shape[0]):
        h = _modulate(x_ref[s], g_ref[...], sh_ref[...], sc_ref[...])
        z = _dot(h.astype(BF16), w_ref[:, K_OFF:K_OFF + 2 * D_KV])
        k_ref[s] = _head_norm(z[:, :D_KV], bd_ref[:D_KV, :D_KV], kg_ref[...]).astype(BF16)
        vt_ref[s] = z[:, D_KV:].T.astype(BF16)


CTX_SAMPLES = 4


def _ctxkv(ctx, csh1, csc1, g_mix, w_in, bd, kg):
    b, n, _ = ctx.shape
    const = lambda i: (0, 0)
    row = lambda i: (i, 0, 0)
    return pl.pallas_call(
        _ctxkv_kernel,
        grid=(b // CTX_SAMPLES,),
        in_specs=[pl.BlockSpec((CTX_SAMPLES, n, D_MODEL), row),
                  pl.BlockSpec((1, D_MODEL), const),
                  pl.BlockSpec((1, D_MODEL), const),
                  pl.BlockSpec((1, D_MODEL), const),
                  pl.BlockSpec((D_MODEL, D_IN), const),
                  pl.BlockSpec(bd.shape, const),
                  pl.BlockSpec((1, D_KV), const)],
        out_specs=[pl.BlockSpec((CTX_SAMPLES, n, D_KV), row),
                   pl.BlockSpec((CTX_SAMPLES, D_KV, n), row)],
        out_shape=[jax.ShapeDtypeStruct((b, n, D_KV), BF16),
                   jax.ShapeDtypeStruct((b, D_KV, n), BF16)],
        compiler_params=_params("arbitrary"),
        name="ctxkv",
    )(ctx, csh1, csc1, g_mix, w_in, bd, kg)


ATTN_LOOKAHEAD = 4
BOUND_MARGIN = 1.02
MAX_SAFE_SCORE_BOUND = 60.0


def _head_slices(q_ref, kl_ref, kc_ref, h):
    kv = slice((h // Q_PER_KV) * HEAD_DIM, (h // Q_PER_KV + 1) * HEAD_DIM)
    qh = q_ref[0, :, h * HEAD_DIM:(h + 1) * HEAD_DIM]
    return qh, kl_ref[0, :, kv], kc_ref[0, :, kv]


def _weighted_values(vlt_ref, vct_ref, p_ref, h, buf, n_lat):
    kv = slice((h // Q_PER_KV) * HEAD_DIM, (h // Q_PER_KV + 1) * HEAD_DIM)
    return (_dot(vlt_ref[0, kv, :], p_ref[buf, :n_lat, :])
            + _dot(vct_ref[0, kv, :], p_ref[buf, n_lat:, :]))


def _attn_kernel(q_ref, kl_ref, kc_ref, vlt_ref, vct_ref, o_ref, st_ref, pt_ref):
    n_lat = kl_ref.shape[1]

    def scores(h):
        qh, kl, kc = _head_slices(q_ref, kl_ref, kc_ref, h)
        st_ref[h, :n_lat, :] = _dot_nt(kl, qh)
        st_ref[h, n_lat:, :] = _dot_nt(kc, qh)

    for h in range(ATTN_LOOKAHEAD):
        scores(h)
    outs = []
    for h in range(N_Q_HEADS):
        if h + ATTN_LOOKAHEAD < N_Q_HEADS:
            scores(h + ATTN_LOOKAHEAD)
        st = st_ref[h]
        pt = jnp.exp2(st - jnp.max(st, axis=0, keepdims=True))
        l = jnp.sum(pt, axis=0, keepdims=True)
        pt_ref[h % 2] = pt.astype(BF16)
        outs.append(_weighted_values(vlt_ref, vct_ref, pt_ref, h, h % 2, n_lat) / l)
    o_ref[0] = jnp.concatenate(outs, axis=0).T.astype(BF16)


def _attn_bounded_kernel(shift_ref, q_ref, kl_ref, kc_ref, vlt_ref, vct_ref, o_ref, pt_ref):
    n_lat = kl_ref.shape[1]
    shift = shift_ref[...]
    sums = [None] * N_Q_HEADS

    def probs(h):
        qh, kl, kc = _head_slices(q_ref, kl_ref, kc_ref, h)
        pl_t = jnp.exp2(_dot_nt(kl, qh) - shift)
        pc_t = jnp.exp2(_dot_nt(kc, qh) - shift)
        sums[h] = jnp.sum(pl_t, axis=0, keepdims=True) + jnp.sum(pc_t, axis=0, keepdims=True)
        pt_ref[h, :n_lat, :] = pl_t.astype(BF16)
        pt_ref[h, n_lat:, :] = pc_t.astype(BF16)

    for h in range(ATTN_LOOKAHEAD):
        probs(h)
    outs = []
    for h in range(N_Q_HEADS):
        if h + ATTN_LOOKAHEAD < N_Q_HEADS:
            probs(h + ATTN_LOOKAHEAD)
        outs.append(_weighted_values(vlt_ref, vct_ref, pt_ref, h, h, n_lat) / sums[h])
    o_ref[0] = jnp.concatenate(outs, axis=0).T.astype(BF16)


def _attn(q, kl, kc, vlt, vct, score_bound, tile_bounded, tile_exact):
    b, n, _ = q.shape
    n_ctx = kc.shape[1]
    nk = n + n_ctx
    per_b = lambda i, j: (i, 0, 0)

    def specs(tile):
        return dict(
            grid=(b, n // tile),
            out_specs=pl.BlockSpec((1, tile, D_ATTN), lambda i, j: (i, j, 0)),
            out_shape=jax.ShapeDtypeStruct((b, n, D_ATTN), BF16),
            compiler_params=_params("arbitrary", "arbitrary"))

    def qkv_specs(tile):
        return [pl.BlockSpec((1, tile, D_ATTN), lambda i, j: (i, j, 0)),
                pl.BlockSpec((1, n, D_KV), per_b),
                pl.BlockSpec((1, n_ctx, D_KV), per_b),
                pl.BlockSpec((1, D_KV, n), per_b),
                pl.BlockSpec((1, D_KV, n_ctx), per_b)]

    def exact_max(score_bound, *qkv):
        t = tile_exact
        return pl.pallas_call(
            _attn_kernel, in_specs=qkv_specs(t),
            scratch_shapes=[pltpu.VMEM((N_Q_HEADS, nk, t), F32), pltpu.VMEM((2, nk, t), BF16)],
            name="attn", **specs(t),
        )(*qkv)

    def bounded(score_bound, *qkv):
        t = tile_bounded
        return pl.pallas_call(
            _attn_bounded_kernel, in_specs=[pl.BlockSpec((1, t), lambda i, j: (0, 0))] + qkv_specs(t),
            scratch_shapes=[pltpu.VMEM((N_Q_HEADS, nk, t), BF16)], name="attn_bounded", **specs(t),
        )(jnp.full((1, t), score_bound, F32), *qkv)

    return lax.cond(score_bound <= MAX_SAFE_SCORE_BOUND, bounded, exact_max,
                    score_bound, q, kl, kc, vlt, vct)


def _mixout_kernel(o_ref, u_ref, vv_ref, x_ref, ws_ref, bs_ref, wout_ref, gt_ref, g_ref,
                   sh_ref, sc_ref, wr_ref, xn_ref, h_ref, aff_ref):
    tile = o_ref.shape[1]
    bs = bs_ref[...]
    gms = []
    for c in range(tile // CHUNK):
        vc = vv_ref[0, c * CHUNK:(c + 1) * CHUNK, :]
        mixed = jnp.concatenate(
            [_dot(ws_ref[g], vc[:, g * GM_GROUP_DIM:(g + 1) * GM_GROUP_DIM]) for g in range(N_GM_GROUPS)],
            axis=-1)
        gms.append(u_ref[0, c * CHUNK:(c + 1) * CHUNK, :] * (mixed + bs))
    gm = jnp.concatenate(gms, axis=0).astype(BF16)

    proj = _dot(o_ref[0], wout_ref[:D_ATTN, :]) + _dot(gm, wout_ref[D_ATTN:, :])
    xn = x_ref[0] + gt_ref[0] * proj
    xn_ref[0] = xn

    h = _modulate(xn, g_ref[...], sh_ref[0], sc_ref[0]).astype(BF16)
    h_ref[0] = h
    logits = _dot_nt(wr_ref[...], h)
    e = jnp.exp(logits - jnp.max(logits, axis=0, keepdims=True))
    aff_ref[0] = e / jnp.sum(e, axis=0, keepdims=True)


def _mixout(o, u, vvn, x, w_s, bs, w_out, gt1, g_ffn, sh2, sc2, w_rt, tile):
    b, n, _ = x.shape
    row = lambda i, j: (i, j, 0)
    per_b = lambda i, j: (i, 0, 0)
    const2 = lambda i, j: (0, 0)
    return pl.pallas_call(
        _mixout_kernel,
        grid=(b, n // tile),
        in_specs=[pl.BlockSpec((1, tile, D_ATTN), row),
                  pl.BlockSpec((1, tile, D_GM), row),
                  pl.BlockSpec((1, tile, D_GM), row),
                  pl.BlockSpec((1, tile, D_MODEL), row),
                  pl.BlockSpec((N_GM_GROUPS, CHUNK, CHUNK), lambda i, j: (0, 0, 0)),
                  pl.BlockSpec((CHUNK, D_GM), const2),
                  pl.BlockSpec((D_ATTN + D_GM, D_MODEL), const2),
                  pl.BlockSpec((1, 1, D_MODEL), per_b),
                  pl.BlockSpec((1, D_MODEL), const2),
                  pl.BlockSpec((1, 1, D_MODEL), per_b),
                  pl.BlockSpec((1, 1, D_MODEL), per_b),
                  pl.BlockSpec((N_EXPERTS, D_MODEL), const2)],
        out_specs=[pl.BlockSpec((1, tile, D_MODEL), row),
                   pl.BlockSpec((1, tile, D_MODEL), row),
                   pl.BlockSpec((1, N_EXPERTS, tile), lambda i, j: (i, 0, j))],
        out_shape=[jax.ShapeDtypeStruct((b, n, D_MODEL), F32),
                   jax.ShapeDtypeStruct((b, n, D_MODEL), BF16),
                   jax.ShapeDtypeStruct((b, N_EXPERTS, n), F32)],
        compiler_params=_params("arbitrary", "arbitrary"),
        name="mixout",
    )(o, u, vvn, x, w_s, bs, w_out, gt1, g_ffn, sh2, sc2, w_rt)


def _cumsum_lanes(m, tri):
    off = jnp.zeros((m.shape[0], 1), F32)
    outs = []
    for blk in range(m.shape[1] // 128):
        mb = m[:, blk * 128:(blk + 1) * 128]
        outs.append(_dot(mb.astype(BF16), tri) + off)
        off = off + jnp.sum(mb, axis=-1, keepdims=True)
    return jnp.concatenate(outs, axis=-1)


REFINE_STEPS = 32


def _route_kernel(cap, aff_ref, tri_ref, pos_ref):
    a = aff_ref[...]
    capf = float(cap)

    def count_ge(t):
        return jnp.sum(jnp.where(a >= t, 1.0, 0.0), axis=-1, keepdims=True)

    def bit_step(i, prefix):
        cand = prefix | jnp.left_shift(jnp.int32(1), 30 - i)
        return jnp.where(count_ge(pltpu.bitcast(cand, F32)) >= capf, cand, prefix)

    prefix = lax.fori_loop(0, 31, bit_step, jnp.zeros((a.shape[0], 1), jnp.int32))
    lo = pltpu.bitcast(prefix, F32)
    hi = pltpu.bitcast(prefix + 1, F32)

    def refine(_, carry):
        lo, hi = carry
        mid = 0.5 * lo + 0.5 * hi
        ok = count_ge(mid) >= capf
        return jnp.where(ok, mid, lo), jnp.where(ok, hi, mid)

    lo, hi = lax.fori_loop(0, REFINE_STEPS, refine, (lo, hi))
    thr = jnp.min(jnp.where(a >= lo, a, jnp.inf), axis=-1, keepdims=True)

    gt = jnp.where(a > thr, 1.0, 0.0)
    eq = jnp.where(a == thr, 1.0, 0.0)
    need = capf - jnp.sum(gt, axis=-1, keepdims=True)
    tri = tri_ref[...]
    sel = gt + eq * jnp.where(_cumsum_lanes(eq, tri) <= need, 1.0, 0.0)
    slot = _cumsum_lanes(sel, tri) - 1.0
    pos_ref[...] = jnp.where(sel > 0.0, slot, -1.0).astype(jnp.int32)


def _route(aff_t, tri, cap, rows):
    b, e, n = aff_t.shape
    pos = pl.pallas_call(
        functools.partial(_route_kernel, cap),
        grid=(b * e // rows,),
        in_specs=[pl.BlockSpec((rows, n), lambda i: (i, 0)),
                  pl.BlockSpec((128, 128), lambda i: (0, 0))],
        out_specs=pl.BlockSpec((rows, n), lambda i: (i, 0)),
        out_shape=jax.ShapeDtypeStruct((b * e, n), jnp.int32),
        compiler_params=_params("arbitrary"),
        name="route",
    )(aff_t.reshape(b * e, n), tri)
    return pos.reshape(b, e, n)


GATHER_EXPERTS = 8


def _gather_kernel(cap, pos_ref, aff_ref, h_ref, xe_ref, gate_ref, tok_ref):
    n = pos_ref.shape[-1]
    slot = lax.broadcasted_iota(jnp.int32, (cap, n), 0)
    tok = (lax.broadcasted_iota(jnp.int32, (cap, n), 1) * SLABS).astype(F32)
    for x in range(GATHER_EXPERTS):
        hit = pos_ref[0, x:x + 1, :] == slot
        xe_ref[x] = _dot(jnp.where(hit, 1.0, 0.0).astype(BF16), h_ref[0]).astype(BF16)
        gate_ref[x] = jnp.sum(jnp.where(hit, aff_ref[0, x:x + 1, :], 0.0), axis=-1, keepdims=True)
        tok_col = jnp.sum(jnp.where(hit, tok, 0.0), axis=-1, keepdims=True)
        tok_row = jnp.broadcast_to(tok_col, (cap, LANES)).T[0:1, :]
        tok_ref[0, x:x + 1, :] = tok_row.astype(jnp.int32)


def _gather(pos_t, aff_t, h2, cap):
    b, e, n = pos_t.shape
    row = lambda i, j: (i, j, 0)
    slots = lambda i, j: (j, i, 0)
    return pl.pallas_call(
        functools.partial(_gather_kernel, cap),
        grid=(b, e // GATHER_EXPERTS),
        in_specs=[pl.BlockSpec((1, GATHER_EXPERTS, n), row),
                  pl.BlockSpec((1, GATHER_EXPERTS, n), row),
                  pl.BlockSpec((1, n, D_MODEL), lambda i, j: (i, 0, 0))],
        out_specs=[pl.BlockSpec((GATHER_EXPERTS, cap, D_MODEL), slots),
                   pl.BlockSpec((GATHER_EXPERTS, cap, 1), slots),
                   pl.BlockSpec((1, GATHER_EXPERTS, cap), lambda i, j: (i, j, 0))],
        out_shape=[jax.ShapeDtypeStruct((e, b * cap, D_MODEL), BF16),
                   jax.ShapeDtypeStruct((e, b * cap, 1), F32),
                   jax.ShapeDtypeStruct((b, e, cap), jnp.int32)],
        compiler_params=_params("arbitrary", "arbitrary"),
        name="gather",
    )(pos_t, aff_t, h2)


FF_CHUNK = 512


def _ffn_kernel(xe_ref, gate_ref, w1_ref, w3_ref, w2_ref, y_ref, w1a, w3a, w2a, w1b, w3b, w2b):
    i = pl.program_id(0)
    j = pl.program_id(1)
    rows13 = w1_ref.shape[1]
    rows2 = w2_ref.shape[1]
    tile = xe_ref.shape[1]
    sets = ((w1a, w3a, w2a), (w1b, w3b, w2b))

    def stage(dst):
        w1s, w3s, w2s = dst
        w1s[pl.ds(j * rows13, rows13), :] = w1_ref[0].astype(BF16)
        w3s[pl.ds(j * rows13, rows13), :] = w3_ref[0].astype(BF16)
        w2s[pl.ds(j * rows2, rows2), :] = w2_ref[0].astype(BF16)

    def compute(src):
        w1s, w3s, w2s = src
        xe = xe_ref[0]
        acc = jnp.zeros((tile, D_MODEL), F32)
        for c in range(EXPERT_FF // FF_CHUNK):
            cols = slice(c * FF_CHUNK, (c + 1) * FF_CHUNK)
            a = _dot(xe, w1s[:, cols])
            g = _dot(xe, w3s[:, cols])
            hid = (a / (1.0 + jnp.exp(-a))) * g
            acc = acc + _dot(hid.astype(BF16), w2s[cols, :])
        y = acc * gate_ref[0]
        for s in range(SLABS):
            y_ref[0, pl.ds(s, tile, stride=SLABS), :] = y[:, s * LANES:(s + 1) * LANES]

    @pl.when(i == 0)
    def _():
        stage(sets[0])
        y_ref[...] = jnp.zeros_like(y_ref)

    for parity in (0, 1):
        @pl.when((i > 0) & (i % 2 == parity))
        def _():
            compute(sets[1 - parity])
            stage(sets[parity])


def _ffn(xe, gate, w1, w3, w2, tile):
    e, m, _ = xe.shape
    steps = m // tile
    cur = lambda i, j: (jnp.maximum(i - 1, 0), j, 0)
    nxt = lambda i, j: (jnp.minimum(i, e - 1), j, 0)
    return pl.pallas_call(
        _ffn_kernel,
        grid=(e + 1, steps),
        in_specs=[pl.BlockSpec((1, tile, D_MODEL), cur),
                  pl.BlockSpec((1, tile, 1), cur),
                  pl.BlockSpec((1, D_MODEL // steps, EXPERT_FF), nxt),
                  pl.BlockSpec((1, D_MODEL // steps, EXPERT_FF), nxt),
                  pl.BlockSpec((1, EXPERT_FF // steps, D_MODEL), nxt)],
        out_specs=pl.BlockSpec((1, tile * SLABS, LANES), lambda i, j: (jnp.where(i == 0, e, i - 1), j, 0)),
        out_shape=jax.ShapeDtypeStruct((e + 1, m * SLABS, LANES), F32),
        scratch_shapes=[pltpu.VMEM((D_MODEL, EXPERT_FF), BF16),
                        pltpu.VMEM((D_MODEL, EXPERT_FF), BF16),
                        pltpu.VMEM((EXPERT_FF, D_MODEL), BF16)] * 2,
        compiler_params=_params("arbitrary", "arbitrary"),
        name="ffn",
    )(xe, gate, w1, w3, w2)


SCATTER_UNROLL = 8
COMBINE_PHASES = 4


def _combine_kernel(cap, n_batch, n_exp, tok_ref, y_ref, xn_ref, gt_ref, g_ref, o_ref, acc_a, acc_b):
    i = pl.program_id(0)
    k = pl.program_id(1)
    experts = n_exp // COMBINE_PHASES
    rows_out = o_ref.shape[1]

    @pl.when((i == 0) & (k == 0))
    def _():
        acc_a[...] = jnp.zeros_like(acc_a)
        acc_b[...] = jnp.zeros_like(acc_b)

    def scatter(acc):
        for x in range(experts):
            base = (i * n_exp + k * experts + x) * cap
            for g in range(cap // SCATTER_UNROLL):
                rows, vals = [], []
                for u in range(SCATTER_UNROLL):
                    r = g * SCATTER_UNROLL + u
                    row = pl.ds(pl.multiple_of(tok_ref[base + r], SLABS), SLABS)
                    rows.append(row)
                    vals.append(acc[row, :] + y_ref[x, r * SLABS:(r + 1) * SLABS, :])
                for row, val in zip(rows, vals):
                    acc[row, :] = val

    def finish(acc):
        start = k * (rows_out * SLABS)
        moe = jnp.concatenate(
            [acc[pl.ds(start + s, rows_out, stride=SLABS), :] for s in range(SLABS)],
            axis=-1)
        z = xn_ref[0] + gt_ref[0] * moe
        o_ref[0] = (z * _rms(z)) * g_ref[...]
        acc[pl.ds(pl.multiple_of(start, SLABS), rows_out * SLABS), :] = (
            jnp.zeros((rows_out * SLABS, LANES), F32))

    accs = (acc_a, acc_b)
    for parity in (0, 1):
        mine, other = accs[parity], accs[1 - parity]

        @pl.when((i % 2 == parity) & (i == 0))
        def _():
            scatter(mine)

        @pl.when((i % 2 == parity) & (i > 0) & (i < n_batch))
        def _():
            finish(other)
            scatter(mine)

        @pl.when((i % 2 == parity) & (i == n_batch))
        def _():
            finish(other)


def _combine(y, tok, xn, gt2, g_final, cap):
    b, n, _ = xn.shape
    e = y.shape[0] - 1
    experts = e // COMBINE_PHASES
    rows_out = n // COMBINE_PHASES
    prev = lambda i, k, tok: (jnp.maximum(i - 1, 0), jnp.where(i == 0, 0, k), 0)
    return pl.pallas_call(
        functools.partial(_combine_kernel, cap, b, e),
        grid_spec=pltpu.PrefetchScalarGridSpec(
            num_scalar_prefetch=1,
            grid=(b + 1, COMBINE_PHASES),
            in_specs=[pl.BlockSpec((experts, cap * SLABS, LANES),
                                   lambda i, k, tok: (k, jnp.minimum(i, b - 1), 0)),
                      pl.BlockSpec((1, rows_out, D_MODEL), prev),
                      pl.BlockSpec((1, 1, D_MODEL), lambda i, k, tok: (jnp.maximum(i - 1, 0), 0, 0)),
                      pl.BlockSpec((1, D_MODEL), lambda i, k, tok: (0, 0))],
            out_specs=pl.BlockSpec((1, rows_out, D_MODEL), prev),
            scratch_shapes=[pltpu.VMEM((n * SLABS, LANES), F32)] * 2),
        out_shape=jax.ShapeDtypeStruct((b, n, D_MODEL), F32),
        compiler_params=_params("arbitrary", "arbitrary"),
        name="combine",
    )(tok, y, xn, gt2, g_final)


def _rope_tables(n):
    pos = jnp.arange(n)
    n_freq = HEAD_DIM // 4
    inv = ROPE_THETA ** (-jnp.arange(n_freq, dtype=F32) / n_freq)
    ang = jnp.concatenate([(pos // GRID_W)[:, None].astype(F32) * inv,
                           (pos % GRID_W)[:, None].astype(F32) * inv], axis=-1)
    cos, sin = jnp.cos(ang), jnp.sin(ang)
    cos_t = jnp.tile(jnp.concatenate([cos, cos], axis=-1), (1, N_Q_HEADS))
    sin_t = jnp.tile(jnp.concatenate([-sin, sin], axis=-1), (1, N_Q_HEADS))
    return cos_t, sin_t


def _block_diag_ones(width, block):
    i = jnp.arange(width) // block
    return (i[:, None] == i[None, :]).astype(BF16)


def kernel(x, c, ctx, c_ctx, w_mod, b_mod, g_mix, g_ffn, w_in, q_gain, k_gain, v_gain, w_s, b_s,
           w_out, w_router, w1, w3, w2, g_final):
    assert w_mod.shape[0] == 1, "single-layer kernel"
    b, n, d = x.shape
    cap = CAPACITY_FACTOR * n // N_EXPERTS

    rows = -(-(b + 1) // 8) * 8
    cvecs = jnp.concatenate([c, c_ctx[None, :], jnp.zeros((rows - b - 1, d), F32)], axis=0)
    mods = _adaln(cvecs, w_mod[0], b_mod[0])
    sh1, sc1, gt1, sh2, sc2, gt2 = [m[:b, None, :] for m in jnp.split(mods, 6, axis=-1)]
    csh1, csc1 = mods[b:b + 1, :d], mods[b:b + 1, d:2 * d]

    cos_t, sin_t = _rope_tables(n)
    bd = _block_diag_ones(256, HEAD_DIM)
    w_in_b = w_in[0].astype(BF16)
    g_mix2 = g_mix[0][None, :]
    qg = jnp.tile(q_gain[0], N_Q_HEADS)[None, :]
    kg = jnp.tile(k_gain[0], N_KV_HEADS)[None, :]

    q, kl, vlt, u, vvn = _inproj(x, sh1, sc1, g_mix2, w_in_b, bd, qg, kg, v_gain[0][None, :],
                                cos_t, sin_t, tile=512)
    kc, vct = _ctxkv(ctx, csh1, csc1, g_mix2, w_in_b, bd, kg)
    score_bound = (BOUND_MARGIN * HEAD_DIM * Q_SCALE) * jnp.max(jnp.abs(q_gain[0])) * jnp.max(jnp.abs(k_gain[0]))
    o = _attn(q, kl, kc, vlt, vct, score_bound, tile_bounded=1024, tile_exact=512)

    bs = jnp.repeat(b_s[0].T, GM_GROUP_DIM, axis=1)
    xn, h2, aff_t = _mixout(o, u, vvn, x, w_s[0].astype(BF16), bs, w_out[0].astype(BF16), gt1,
                            g_ffn[0][None, :], sh2, sc2, w_router[0].T.astype(BF16), tile=512)

    tri = (jnp.arange(128)[:, None] <= jnp.arange(128)[None, :]).astype(BF16)
    pos_t = _route(aff_t, tri, cap, rows=256)
    xe, gate, tok = _gather(pos_t, aff_t, h2, cap)
    y = _ffn(xe, gate, w1[0], w3[0], w2[0], tile=512)
    return _combine(y, tok.reshape(-1), xn, gt2, g_final[None, :], cap)
```

```python
import functools
import math

import jax
import jax.numpy as jnp
from jax import lax
from jax.experimental import pallas as pl
from jax.experimental.pallas import tpu as pltpu

D_MODEL = 1024
CTX_LEN = 256
GRID_W = 64
HEAD_DIM = 64
N_Q_HEADS = 8
N_KV_HEADS = 2
Q_PER_KV = N_Q_HEADS // N_KV_HEADS
D_ATTN = N_Q_HEADS * HEAD_DIM
D_KV = N_KV_HEADS * HEAD_DIM
N_GM_GROUPS = 8
GM_GROUP_DIM = 64
D_GM = N_GM_GROUPS * GM_GROUP_DIM
CHUNK = 128
D_IN = D_ATTN + 2 * D_KV + 2 * D_GM
ROPE_THETA = 10000.0
N_EXPERTS = 16
CAPACITY_FACTOR = 2
EXPERT_FF = 2048
EPS = 1e-6

F32 = jnp.float32
BF16 = jnp.bfloat16

VMEM_LIMIT_BYTES = 52 * 1024 * 1024
SUB_ROWS = 256
LANES = 128
MXU_WIDTH = 256
SLABS = D_MODEL // LANES

K_OFF = D_ATTN
V_OFF = D_ATTN + D_KV
GM_OFF = D_ATTN + 2 * D_KV

Q_SCALE = math.log2(math.e) * HEAD_DIM ** -0.5


def _params(*sem):
    return pltpu.CompilerParams(dimension_semantics=sem, vmem_limit_bytes=VMEM_LIMIT_BYTES)


def _dot(a, b):
    return jnp.dot(a, b, preferred_element_type=F32)


def _dot_nt(a, b):
    return lax.dot_general(a, b, (((1,), (1,)), ((), ())), preferred_element_type=F32)


def _rms(v):
    return lax.rsqrt(jnp.sum(v * v, axis=-1, keepdims=True) * (1.0 / v.shape[-1]) + EPS)


def _modulate(x, g, shift, scale):
    return ((x * _rms(x)) * g) * (1.0 + scale) + shift


def _seg_sum(v, bd):
    w = bd.shape[0]
    outs = []
    for j in range(v.shape[1] // w):
        c = v[:, j * w:(j + 1) * w]
        hi = c.astype(BF16)
        lo = (c - hi.astype(F32)).astype(BF16)
        outs.append(_dot(hi, bd) + _dot(lo, bd))
    return outs[0] if len(outs) == 1 else jnp.concatenate(outs, axis=-1)


def _head_norm(z, bd, gain):
    ss = _seg_sum(z * z, bd)
    return (z * lax.rsqrt(ss * (1.0 / HEAD_DIM) + EPS)) * gain


def _rope(v, cos, sin_signed):
    w = v.shape[-1]
    lane = lax.broadcasted_iota(jnp.int32, v.shape, 1)
    partner = jnp.where((lane & (HEAD_DIM // 2)) == 0,
                        pltpu.roll(v, w - HEAD_DIM // 2, axis=1),
                        pltpu.roll(v, HEAD_DIM // 2, axis=1))
    return v * cos + partner * sin_signed


def _gelu(z):
    return 0.5 * z * (1.0 + lax.erf(z * math.sqrt(0.5)))


def _adaln_kernel(c_ref, w_ref, b_ref, o_ref):
    c = c_ref[...]
    s = c / (1.0 + jnp.exp(-c))
    o_ref[...] = _dot(s.astype(BF16), w_ref[...].astype(BF16)) + b_ref[...]


def _adaln(cvecs, w_mod, b_mod):
    rows = cvecs.shape[0]
    n_out = w_mod.shape[1]
    tn = 768
    return pl.pallas_call(
        _adaln_kernel,
        grid=(n_out // tn,),
        in_specs=[pl.BlockSpec((rows, D_MODEL), lambda j: (0, 0)),
                  pl.BlockSpec((D_MODEL, tn), lambda j: (0, j)),
                  pl.BlockSpec((1, tn), lambda j: (0, j))],
        out_specs=pl.BlockSpec((rows, tn), lambda j: (0, j)),
        out_shape=jax.ShapeDtypeStruct((rows, n_out), F32),
        compiler_params=_params("arbitrary"),
        name="adaln",
    )(cvecs, w_mod, b_mod.reshape(1, n_out))


def _inproj_kernel(x_ref, sh_ref, sc_ref, g_ref, w_ref, bd_ref, qg_ref, kg_ref, vg_ref,
                   cos_ref, sin_ref, q_ref, k_ref, vt_ref, u_ref, vv_ref):
    bd = bd_ref[...]
    for r0 in range(0, x_ref.shape[1], SUB_ROWS):
        rows = slice(r0, r0 + SUB_ROWS)
        h = _modulate(x_ref[0, rows, :], g_ref[...], sh_ref[0], sc_ref[0])
        z = _dot(h.astype(BF16), w_ref[...])
        cos = cos_ref[rows, :]
        sin = sin_ref[rows, :]

        q = _rope(_head_norm(z[:, :D_ATTN], bd, qg_ref[...]), cos, sin)
        q_ref[0, rows, :] = (q * Q_SCALE).astype(BF16)

        k = _head_norm(z[:, K_OFF:K_OFF + D_KV], bd[:D_KV, :D_KV], kg_ref[...])
        k_ref[0, rows, :] = _rope(k, cos[:, :D_KV], sin[:, :D_KV]).astype(BF16)
        vt_ref[0, :, rows] = z[:, V_OFF:V_OFF + D_KV].T.astype(BF16)

        gz = _gelu(z[:, GM_OFF:])
        u_ref[0, rows, :] = gz[:, :D_GM].astype(BF16)
        vv = gz[:, D_GM:]
        vv_ref[0, rows, :] = ((vv * _rms(vv)) * vg_ref[...]).astype(BF16)


def _inproj(x, sh1, sc1, g_mix, w_in, bd, qg, kg, vg, cos_t, sin_t, tile):
    b, n, _ = x.shape
    row = lambda i, j: (j, i, 0)
    per_b = lambda i, j: (j, 0, 0)
    const = lambda i, j: (0, 0)
    tab = lambda i, j: (i, 0)
    return pl.pallas_call(
        _inproj_kernel,
        grid=(n // tile, b),
        in_specs=[pl.BlockSpec((1, tile, D_MODEL), row),
                  pl.BlockSpec((1, 1, D_MODEL), per_b),
                  pl.BlockSpec((1, 1, D_MODEL), per_b),
                  pl.BlockSpec((1, D_MODEL), const),
                  pl.BlockSpec((D_MODEL, D_IN), const),
                  pl.BlockSpec(bd.shape, const),
                  pl.BlockSpec((1, D_ATTN), const),
                  pl.BlockSpec((1, D_KV), const),
                  pl.BlockSpec((1, D_GM), const),
                  pl.BlockSpec((tile, D_ATTN), tab),
                  pl.BlockSpec((tile, D_ATTN), tab)],
        out_specs=[pl.BlockSpec((1, tile, D_ATTN), row),
                   pl.BlockSpec((1, tile, D_KV), row),
                   pl.BlockSpec((1, D_KV, tile), lambda i, j: (j, 0, i)),
                   pl.BlockSpec((1, tile, D_GM), row),
                   pl.BlockSpec((1, tile, D_GM), row)],
        out_shape=[jax.ShapeDtypeStruct((b, n, D_ATTN), BF16),
                   jax.ShapeDtypeStruct((b, n, D_KV), BF16),
                   jax.ShapeDtypeStruct((b, D_KV, n), BF16),
                   jax.ShapeDtypeStruct((b, n, D_GM), BF16),
                   jax.ShapeDtypeStruct((b, n, D_GM), BF16)],
        compiler_params=_params("arbitrary", "arbitrary"),
        name="inproj",
    )(x, sh1, sc1, g_mix, w_in, bd, qg, kg, vg, cos_t, sin_t)


def _ctxkv_kernel(x_ref, sh_ref, sc_ref, g_ref, w_ref, bd_ref, kg_ref, k_ref, vt_ref):
    for s in range(x_ref.shape[0]):
        h = _modulate(x_ref[s], g_ref[...], sh_ref[...], sc_ref[...])
        z = _dot(h.astype(BF16), w_ref[:, K_OFF:K_OFF + 2 * D_KV])
        k_ref[s] = _head_norm(z[:, :D_KV], bd_ref[:D_KV, :D_KV], kg_ref[...]).astype(BF16)
        vt_ref[s] = z[:, D_KV:].T.astype(BF16)


CTX_SAMPLES = 4


def _ctxkv(ctx, csh1, csc1, g_mix, w_in, bd, kg):
    b, n, _ = ctx.shape
    const = lambda i: (0, 0)
    row = lambda i: (i, 0, 0)
    return pl.pallas_call(
        _ctxkv_kernel,
        grid=(b // CTX_SAMPLES,),
        in_specs=[pl.BlockSpec((CTX_SAMPLES, n, D_MODEL), row),
                  pl.BlockSpec((1, D_MODEL), const),
                  pl.BlockSpec((1, D_MODEL), const),
                  pl.BlockSpec((1, D_MODEL), const),
                  pl.BlockSpec((D_MODEL, D_IN), const),
                  pl.BlockSpec(bd.shape, const),
                  pl.BlockSpec((1, D_KV), const)],
        out_specs=[pl.BlockSpec((CTX_SAMPLES, n, D_KV), row),
                   pl.BlockSpec((CTX_SAMPLES, D_KV, n), row)],
        out_shape=[jax.ShapeDtypeStruct((b, n, D_KV), BF16),
                   jax.ShapeDtypeStruct((b, D_KV, n), BF16)],
        compiler_params=_params("arbitrary"),
        name="ctxkv",
    )(ctx, csh1, csc1, g_mix, w_in, bd, kg)


ATTN_LOOKAHEAD = 4
BOUND_MARGIN = 1.02
MAX_SAFE_SCORE_BOUND = 60.0


def _head_slices(q_ref, kl_ref, kc_ref, h):
    kv = slice((h // Q_PER_KV) * HEAD_DIM, (h // Q_PER_KV + 1) * HEAD_DIM)
    qh = q_ref[0, :, h * HEAD_DIM:(h + 1) * HEAD_DIM]
    return qh, kl_ref[0, :, kv], kc_ref[0, :, kv]


def _weighted_values(vlt_ref, vct_ref, p_ref, h, buf, n_lat):
    kv = slice((h // Q_PER_KV) * HEAD_DIM, (h // Q_PER_KV + 1) * HEAD_DIM)
    return (_dot(vlt_ref[0, kv, :], p_ref[buf, :n_lat, :])
            + _dot(vct_ref[0, kv, :], p_ref[buf, n_lat:, :]))


def _attn_kernel(q_ref, kl_ref, kc_ref, vlt_ref, vct_ref, o_ref, st_ref, pt_ref):
    n_lat = kl_ref.shape[1]

    def scores(h):
        qh, kl, kc = _head_slices(q_ref, kl_ref, kc_ref, h)
        st_ref[h, :n_lat, :] = _dot_nt(kl, qh)
        st_ref[h, n_lat:, :] = _dot_nt(kc, qh)

    for h in range(ATTN_LOOKAHEAD):
        scores(h)
    outs = []
    for h in range(N_Q_HEADS):
        if h + ATTN_LOOKAHEAD < N_Q_HEADS:
            scores(h + ATTN_LOOKAHEAD)
        st = st_ref[h]
        pt = jnp.exp2(st - jnp.max(st, axis=0, keepdims=True))
        l = jnp.sum(pt, axis=0, keepdims=True)
        pt_ref[h % 2] = pt.astype(BF16)
        outs.append(_weighted_values(vlt_ref, vct_ref, pt_ref, h, h % 2, n_lat) / l)
    o_ref[0] = jnp.concatenate(outs, axis=0).T.astype(BF16)


def _attn_bounded_kernel(shift_ref, q_ref, kl_ref, kc_ref, vlt_ref, vct_ref, o_ref, pt_ref):
    n_lat = kl_ref.shape[1]
    shift = shift_ref[...]
    sums = [None] * N_Q_HEADS

    def probs(h):
        qh, kl, kc = _head_slices(q_ref, kl_ref, kc_ref, h)
        pl_t = jnp.exp2(_dot_nt(kl, qh) - shift)
        pc_t = jnp.exp2(_dot_nt(kc, qh) - shift)
        sums[h] = jnp.sum(pl_t, axis=0, keepdims=True) + jnp.sum(pc_t, axis=0, keepdims=True)
        pt_ref[h, :n_lat, :] = pl_t.astype(BF16)
        pt_ref[h, n_lat:, :] = pc_t.astype(BF16)

    for h in range(ATTN_LOOKAHEAD):
        probs(h)
    outs = []
    for h in range(N_Q_HEADS):
        if h + ATTN_LOOKAHEAD < N_Q_HEADS:
            probs(h + ATTN_LOOKAHEAD)
        outs.append(_weighted_values(vlt_ref, vct_ref, pt_ref, h, h, n_lat) / sums[h])
    o_ref[0] = jnp.concatenate(outs, axis=0).T.astype(BF16)


def _attn(q, kl, kc, vlt, vct, score_bound, tile_bounded, tile_exact):
    b, n, _ = q.shape
    n_ctx = kc.shape[1]
    nk = n + n_ctx
    per_b = lambda i, j: (i, 0, 0)

    def specs(tile):
        return dict(
            grid=(b, n // tile),
            out_specs=pl.BlockSpec((1, tile, D_ATTN), lambda i, j: (i, j, 0)),
            out_shape=jax.ShapeDtypeStruct((b, n, D_ATTN), BF16),
            compiler_params=_params("arbitrary", "arbitrary"))

    def qkv_specs(tile):
        return [pl.BlockSpec((1, tile, D_ATTN), lambda i, j: (i, j, 0)),
                pl.BlockSpec((1, n, D_KV), per_b),
                pl.BlockSpec((1, n_ctx, D_KV), per_b),
                pl.BlockSpec((1, D_KV, n), per_b),
                pl.BlockSpec((1, D_KV, n_ctx), per_b)]

    def exact_max(score_bound, *qkv):
        t = tile_exact
        return pl.pallas_call(
            _attn_kernel, in_specs=qkv_specs(t),
            scratch_shapes=[pltpu.VMEM((N_Q_HEADS, nk, t), F32), pltpu.VMEM((2, nk, t), BF16)],
            name="attn", **specs(t),
        )(*qkv)

    def bounded(score_bound, *qkv):
        t = tile_bounded
        return pl.pallas_call(
            _attn_bounded_kernel, in_specs=[pl.BlockSpec((1, t), lambda i, j: (0, 0))] + qkv_specs(t),
            scratch_shapes=[pltpu.VMEM((N_Q_HEADS, nk, t), BF16)], name="attn_bounded", **specs(t),
        )(jnp.full((1, t), score_bound, F32), *qkv)

    return lax.cond(score_bound <= MAX_SAFE_SCORE_BOUND, bounded, exact_max,
                    score_bound, q, kl, kc, vlt, vct)


def _mixout_kernel(o_ref, u_ref, vv_ref, x_ref, ws_ref, bs_ref, wout_ref, gt_ref, g_ref,
                   sh_ref, sc_ref, wr_ref, xn_ref, h_ref, aff_ref):
    tile = o_ref.shape[1]
    bs = bs_ref[...]
    gms = []
    for c in range(tile // CHUNK):
        vc = vv_ref[0, c * CHUNK:(c + 1) * CHUNK, :]
        mixed = jnp.concatenate(
            [_dot(ws_ref[g], vc[:, g * GM_GROUP_DIM:(g + 1) * GM_GROUP_DIM]) for g in range(N_GM_GROUPS)],
            axis=-1)
        gms.append(u_ref[0, c * CHUNK:(c + 1) * CHUNK, :] * (mixed + bs))
    gm = jnp.concatenate(gms, axis=0).astype(BF16)

    proj = _dot(o_ref[0], wout_ref[:D_ATTN, :]) + _dot(gm, wout_ref[D_ATTN:, :])
    xn = x_ref[0] + gt_ref[0] * proj
    xn_ref[0] = xn

    h = _modulate(xn, g_ref[...], sh_ref[0], sc_ref[0]).astype(BF16)
    h_ref[0] = h
    logits = _dot_nt(wr_ref[...], h)
    e = jnp.exp(logits - jnp.max(logits, axis=0, keepdims=True))
    aff_ref[0] = e / jnp.sum(e, axis=0, keepdims=True)


def _mixout(o, u, vvn, x, w_s, bs, w_out, gt1, g_ffn, sh2, sc2, w_rt, tile):
    b, n, _ = x.shape
    row = lambda i, j: (i, j, 0)
    per_b = lambda i, j: (i, 0, 0)
    const2 = lambda i, j: (0, 0)
    return pl.pallas_call(
        _mixout_kernel,
        grid=(b, n // tile),
        in_specs=[pl.BlockSpec((1, tile, D_ATTN), row),
                  pl.BlockSpec((1, tile, D_GM), row),
                  pl.BlockSpec((1, tile, D_GM), row),
                  pl.BlockSpec((1, tile, D_MODEL), row),
                  pl.BlockSpec((N_GM_GROUPS, CHUNK, CHUNK), lambda i, j: (0, 0, 0)),
                  pl.BlockSpec((CHUNK, D_GM), const2),
                  pl.BlockSpec((D_ATTN + D_GM, D_MODEL), const2),
                  pl.BlockSpec((1, 1, D_MODEL), per_b),
                  pl.BlockSpec((1, D_MODEL), const2),
                  pl.BlockSpec((1, 1, D_MODEL), per_b),
                  pl.BlockSpec((1, 1, D_MODEL), per_b),
                  pl.BlockSpec((N_EXPERTS, D_MODEL), const2)],
        out_specs=[pl.BlockSpec((1, tile, D_MODEL), row),
                   pl.BlockSpec((1, tile, D_MODEL), row),
                   pl.BlockSpec((1, N_EXPERTS, tile), lambda i, j: (i, 0, j))],
        out_shape=[jax.ShapeDtypeStruct((b, n, D_MODEL), F32),
                   jax.ShapeDtypeStruct((b, n, D_MODEL), BF16),
                   jax.ShapeDtypeStruct((b, N_EXPERTS, n), F32)],
        compiler_params=_params("arbitrary", "arbitrary"),
        name="mixout",
    )(o, u, vvn, x, w_s, bs, w_out, gt1, g_ffn, sh2, sc2, w_rt)


def _cumsum_lanes(m, tri):
    off = jnp.zeros((m.shape[0], 1), F32)
    outs = []
    for blk in range(m.shape[1] // 128):
        mb = m[:, blk * 128:(blk + 1) * 128]
        outs.append(_dot(mb.astype(BF16), tri) + off)
        off = off + jnp.sum(mb, axis=-1, keepdims=True)
    return jnp.concatenate(outs, axis=-1)


REFINE_STEPS = 32


def _route_kernel(cap, aff_ref, tri_ref, pos_ref):
    a = aff_ref[...]
    capf = float(cap)

    def count_ge(t):
        return jnp.sum(jnp.where(a >= t, 1.0, 0.0), axis=-1, keepdims=True)

    def bit_step(i, prefix):
        cand = prefix | jnp.left_shift(jnp.int32(1), 30 - i)
        return jnp.where(count_ge(pltpu.bitcast(cand, F32)) >= capf, cand, prefix)

    prefix = lax.fori_loop(0, 31, bit_step, jnp.zeros((a.shape[0], 1), jnp.int32))
    lo = pltpu.bitcast(prefix, F32)
    hi = pltpu.bitcast(prefix + 1, F32)

    def refine(_, carry):
        lo, hi = carry
        mid = 0.5 * lo + 0.5 * hi
        ok = count_ge(mid) >= capf
        return jnp.where(ok, mid, lo), jnp.where(ok, hi, mid)

    lo, hi = lax.fori_loop(0, REFINE_STEPS, refine, (lo, hi))
    thr = jnp.min(jnp.where(a >= lo, a, jnp.inf), axis=-1, keepdims=True)

    gt = jnp.where(a > thr, 1.0, 0.0)
    eq = jnp.where(a == thr, 1.0, 0.0)
    need = capf - jnp.sum(gt, axis=-1, keepdims=True)
    tri = tri_ref[...]
    sel = gt + eq * jnp.where(_cumsum_lanes(eq, tri) <= need, 1.0, 0.0)
    slot = _cumsum_lanes(sel, tri) - 1.0
    pos_ref[...] = jnp.where(sel > 0.0, slot, -1.0).astype(jnp.int32)


def _route(aff_t, tri, cap, rows):
    b, e, n = aff_t.shape
    pos = pl.pallas_call(
        functools.partial(_route_kernel, cap),
        grid=(b * e // rows,),
        in_specs=[pl.BlockSpec((rows, n), lambda i: (i, 0)),
                  pl.BlockSpec((128, 128), lambda i: (0, 0))],
        out_specs=pl.BlockSpec((rows, n), lambda i: (i, 0)),
        out_shape=jax.ShapeDtypeStruct((b * e, n), jnp.int32),
        compiler_params=_params("arbitrary"),
        name="route",
    )(aff_t.reshape(b * e, n), tri)
    return pos.reshape(b, e, n)


GATHER_EXPERTS = 16


def _gather_kernel(cap, pos_ref, aff_ref, h_ref, xe_ref, gate_ref, tok_ref):
    n = pos_ref.shape[-1]
    slot = lax.broadcasted_iota(jnp.int32, (cap, n), 0)
    tok = (lax.broadcasted_iota(jnp.int32, (cap, n), 1) * SLABS).astype(F32)
    for x in range(GATHER_EXPERTS):
        hit = pos_ref[0, x:x + 1, :] == slot
        xe_ref[x] = _dot(jnp.where(hit, 1.0, 0.0).astype(BF16), h_ref[0]).astype(BF16)
        gate_ref[x] = jnp.sum(jnp.where(hit, aff_ref[0, x:x + 1, :], 0.0), axis=-1, keepdims=True)
        tok_col = jnp.sum(jnp.where(hit, tok, 0.0), axis=-1, keepdims=True)
        tok_row = jnp.broadcast_to(tok_col, (cap, LANES)).T[0:1, :]
        tok_ref[0, x:x + 1, :] = tok_row.astype(jnp.int32)


def _gather(pos_t, aff_t, h2, cap):
    b, e, n = pos_t.shape
    row = lambda i, j: (i, j, 0)
    slots = lambda i, j: (j, i, 0)
    return pl.pallas_call(
        functools.partial(_gather_kernel, cap),
        grid=(b, e // GATHER_EXPERTS),
        in_specs=[pl.BlockSpec((1, GATHER_EXPERTS, n), row),
                  pl.BlockSpec((1, GATHER_EXPERTS, n), row),
                  pl.BlockSpec((1, n, D_MODEL), lambda i, j: (i, 0, 0))],
        out_specs=[pl.BlockSpec((GATHER_EXPERTS, cap, D_MODEL), slots),
                   pl.BlockSpec((GATHER_EXPERTS, cap, 1), slots),
                   pl.BlockSpec((1, GATHER_EXPERTS, cap), lambda i, j: (i, j, 0))],
        out_shape=[jax.ShapeDtypeStruct((e, b * cap, D_MODEL), BF16),
                   jax.ShapeDtypeStruct((e, b * cap, 1), F32),
                   jax.ShapeDtypeStruct((b, e, cap), jnp.int32)],
        compiler_params=_params("arbitrary", "arbitrary"),
        name="gather",
    )(pos_t, aff_t, h2)


FF_CHUNK = 512


def _ffn_kernel(xe_ref, gate_ref, w1_ref, w3_ref, w2_ref, y_ref, w1a, w3a, w2a, w1b, w3b, w2b):
    i = pl.program_id(0)
    j = pl.program_id(1)
    rows13 = w1_ref.shape[1]
    rows2 = w2_ref.shape[1]
    tile = xe_ref.shape[1]
    sets = ((w1a, w3a, w2a), (w1b, w3b, w2b))

    def stage(dst):
        w1s, w3s, w2s = dst
        w1s[pl.ds(j * rows13, rows13), :] = w1_ref[0].astype(BF16)
        w3s[pl.ds(j * rows13, rows13), :] = w3_ref[0].astype(BF16)
        w2s[pl.ds(j * rows2, rows2), :] = w2_ref[0].astype(BF16)

    def compute(src):
        w1s, w3s, w2s = src
        xe = xe_ref[0]
        acc = jnp.zeros((tile, D_MODEL), F32)
        for c in range(EXPERT_FF // FF_CHUNK):
            cols = slice(c * FF_CHUNK, (c + 1) * FF_CHUNK)
            a = _dot(xe, w1s[:, cols])
            g = _dot(xe, w3s[:, cols])
            hid = (a / (1.0 + jnp.exp(-a))) * g
            acc = acc + _dot(hid.astype(BF16), w2s[cols, :])
        y = acc * gate_ref[0]
        for s in range(SLABS):
            y_ref[0, pl.ds(s, tile, stride=SLABS), :] = y[:, s * LANES:(s + 1) * LANES]

    @pl.when(i == 0)
    def _():
        stage(sets[0])
        y_ref[...] = jnp.zeros_like(y_ref)

    for parity in (0, 1):
        @pl.when((i > 0) & (i % 2 == parity))
        def _():
            compute(sets[1 - parity])
            stage(sets[parity])


def _ffn(xe, gate, w1, w3, w2, tile):
    e, m, _ = xe.shape
    steps = m // tile
    cur = lambda i, j: (jnp.maximum(i - 1, 0), j, 0)
    nxt = lambda i, j: (jnp.minimum(i, e - 1), j, 0)
    return pl.pallas_call(
        _ffn_kernel,
        grid=(e + 1, steps),
        in_specs=[pl.BlockSpec((1, tile, D_MODEL), cur),
                  pl.BlockSpec((1, tile, 1), cur),
                  pl.BlockSpec((1, D_MODEL // steps, EXPERT_FF), nxt),
                  pl.BlockSpec((1, D_MODEL // steps, EXPERT_FF), nxt),
                  pl.BlockSpec((1, EXPERT_FF // steps, D_MODEL), nxt)],
        out_specs=pl.BlockSpec((1, tile * SLABS, LANES), lambda i, j: (jnp.where(i == 0, e, i - 1), j, 0)),
        out_shape=jax.ShapeDtypeStruct((e + 1, m * SLABS, LANES), F32),
        scratch_shapes=[pltpu.VMEM((D_MODEL, EXPERT_FF), BF16),
                        pltpu.VMEM((D_MODEL, EXPERT_FF), BF16),
                        pltpu.VMEM((EXPERT_FF, D_MODEL), BF16)] * 2,
        compiler_params=_params("arbitrary", "arbitrary"),
        name="ffn",
    )(xe, gate, w1, w3, w2)


SCATTER_UNROLL = 8
COMBINE_PHASES = 4


def _combine_kernel(cap, n_batch, n_exp, tok_ref, y_ref, xn_ref, gt_ref, g_ref, o_ref, acc_a, acc_b):
    i = pl.program_id(0)
    k = pl.program_id(1)
    experts = n_exp // COMBINE_PHASES
    rows_out = o_ref.shape[1]

    @pl.when((i == 0) & (k == 0))
    def _():
        acc_a[...] = jnp.zeros_like(acc_a)
        acc_b[...] = jnp.zeros_like(acc_b)

    def scatter(acc):
        for x in range(experts):
            base = (i * n_exp + k * experts + x) * cap
            for g in range(cap // SCATTER_UNROLL):
                rows, vals = [], []
                for u in range(SCATTER_UNROLL):
                    r = g * SCATTER_UNROLL + u
                    row = pl.ds(pl.multiple_of(tok_ref[base + r], SLABS), SLABS)
                    rows.append(row)
                    vals.append(acc[row, :] + y_ref[x, r * SLABS:(r + 1) * SLABS, :])
                for row, val in zip(rows, vals):
                    acc[row, :] = val

    def finish(acc):
        start = k * (rows_out * SLABS)
        moe = jnp.concatenate(
            [acc[pl.ds(start + s, rows_out, stride=SLABS), :] for s in range(SLABS)],
            axis=-1)
        z = xn_ref[0] + gt_ref[0] * moe
        o_ref[0] = (z * _rms(z)) * g_ref[...]
        acc[pl.ds(pl.multiple_of(start, SLABS), rows_out * SLABS), :] = (
            jnp.zeros((rows_out * SLABS, LANES), F32))

    accs = (acc_a, acc_b)
    for parity in (0, 1):
        mine, other = accs[parity], accs[1 - parity]

        @pl.when((i % 2 == parity) & (i == 0))
        def _():
            scatter(mine)

        @pl.when((i % 2 == parity) & (i > 0) & (i < n_batch))
        def _():
            finish(other)
            scatter(mine)

        @pl.when((i % 2 == parity) & (i == n_batch))
        def _():
            finish(other)


def _combine(y, tok, xn, gt2, g_final, cap):
    b, n, _ = xn.shape
    e = y.shape[0] - 1
    experts = e // COMBINE_PHASES
    rows_out = n // COMBINE_PHASES
    prev = lambda i, k, tok: (jnp.maximum(i - 1, 0), jnp.where(i == 0, 0, k), 0)
    return pl.pallas_call(
        functools.partial(_combine_kernel, cap, b, e),
        grid_spec=pltpu.PrefetchScalarGridSpec(
            num_scalar_prefetch=1,
            grid=(b + 1, COMBINE_PHASES),
            in_specs=[pl.BlockSpec((experts, cap * SLABS, LANES),
                                   lambda i, k, tok: (k, jnp.minimum(i, b - 1), 0)),
                      pl.BlockSpec((1, rows_out, D_MODEL), prev),
                      pl.BlockSpec((1, 1, D_MODEL), lambda i, k, tok: (jnp.maximum(i - 1, 0), 0, 0)),
                      pl.BlockSpec((1, D_MODEL), lambda i, k, tok: (0, 0))],
            out_specs=pl.BlockSpec((1, rows_out, D_MODEL), prev),
            scratch_shapes=[pltpu.VMEM((n * SLABS, LANES), F32)] * 2),
        out_shape=jax.ShapeDtypeStruct((b, n, D_MODEL), F32),
        compiler_params=_params("arbitrary", "arbitrary"),
        name="combine",
    )(tok, y, xn, gt2, g_final)


def _rope_tables(n):
    pos = jnp.arange(n)
    n_freq = HEAD_DIM // 4
    inv = ROPE_THETA ** (-jnp.arange(n_freq, dtype=F32) / n_freq)
    ang = jnp.concatenate([(pos // GRID_W)[:, None].astype(F32) * inv,
                           (pos % GRID_W)[:, None].astype(F32) * inv], axis=-1)
    cos, sin = jnp.cos(ang), jnp.sin(ang)
    cos_t = jnp.tile(jnp.concatenate([cos, cos], axis=-1), (1, N_Q_HEADS))
    sin_t = jnp.tile(jnp.concatenate([-sin, sin], axis=-1), (1, N_Q_HEADS))
    return cos_t, sin_t


def _block_diag_ones(width, block):
    i = jnp.arange(width) // block
    return (i[:, None] == i[None, :]).astype(BF16)


def kernel(x, c, ctx, c_ctx, w_mod, b_mod, g_mix, g_ffn, w_in, q_gain, k_gain, v_gain, w_s, b_s,
           w_out, w_router, w1, w3, w2, g_final):
    assert w_mod.shape[0] == 1, "single-layer kernel"
    b, n, d = x.shape
    cap = CAPACITY_FACTOR * n // N_EXPERTS

    rows = -(-(b + 1) // 8) * 8
    cvecs = jnp.concatenate([c, c_ctx[None, :], jnp.zeros((rows - b - 1, d), F32)], axis=0)
    mods = _adaln(cvecs, w_mod[0], b_mod[0])
    sh1, sc1, gt1, sh2, sc2, gt2 = [m[:b, None, :] for m in jnp.split(mods, 6, axis=-1)]
    csh1, csc1 = mods[b:b + 1, :d], mods[b:b + 1, d:2 * d]

    cos_t, sin_t = _rope_tables(n)
    bd = _block_diag_ones(MXU_WIDTH, HEAD_DIM)
    w_in_b = w_in[0].astype(BF16)
    g_mix2 = g_mix[0][None, :]
    qg = jnp.tile(q_gain[0], N_Q_HEADS)[None, :]
    kg = jnp.tile(k_gain[0], N_KV_HEADS)[None, :]

    q, kl, vlt, u, vvn = _inproj(x, sh1, sc1, g_mix2, w_in_b, bd, qg, kg, v_gain[0][None, :],
                                cos_t, sin_t, tile=1024)
    kc, vct = _ctxkv(ctx, csh1, csc1, g_mix2, w_in_b, bd, kg)
    score_bound = (BOUND_MARGIN * HEAD_DIM * Q_SCALE) * jnp.max(jnp.abs(q_gain[0])) * jnp.max(jnp.abs(k_gain[0]))
    o = _attn(q, kl, kc, vlt, vct, score_bound, tile_bounded=1024, tile_exact=512)

    bs = jnp.repeat(b_s[0].T, GM_GROUP_DIM, axis=1)
    xn, h2, aff_t = _mixout(o, u, vvn, x, w_s[0].astype(BF16), bs, w_out[0].astype(BF16), gt1,
                            g_ffn[0][None, :], sh2, sc2, w_router[0].T.astype(BF16), tile=1024)

    tri = (jnp.arange(LANES)[:, None] <= jnp.arange(LANES)[None, :]).astype(BF16)
    pos_t = _route(aff_t, tri, cap, rows=256)
    xe, gate, tok = _gather(pos_t, aff_t, h2, cap)
    y = _ffn(xe, gate, w1[0], w3[0], w2[0], tile=512)
    return _combine(y, tok.reshape(-1), xn, gt2, g_final[None, :], cap)
```
